```python
import jax
import jax.numpy as jnp
from jax import lax

D_MODEL = 1024
BATCH = 4
SEQ = 8192
DEPTH = 2

MLA_HEADS = 8
MLA_Q_RANK = 256
MLA_KV_RANK = 128
MLA_NOPE = 64
MLA_ROPE = 32
MLA_V = 64
MLA_QK = MLA_NOPE + MLA_ROPE
ROPE_THETA = 10000.0
ATTN_Q_BLOCK = 128

GDN_HEADS = 8
GDN_DK = 64
GDN_DV = 64
GDN_CONV = 4
GDN_CHUNK = 64
GDN_CONV_CH = 2 * GDN_HEADS * GDN_DK + GDN_HEADS * GDN_DV

MOBA_HEADS = 8
MOBA_DH = 64
MOBA_BLOCK = 256
MOBA_TOPK = 3
MOBA_Q_CHUNK = 32

N_BRANCH = 3
BRANCH_W = 512

PEER_HEADS = 8
PEER_NKEYS = 128
PEER_EXPERTS = PEER_NKEYS * PEER_NKEYS
PEER_DQ = 256
PEER_TOPK = 16
PEER_TOK_CHUNK = 128

NORM_EPS = 1e-6
NEG_INF = -1e30

SPLIT_SIZES = (
    MLA_Q_RANK, MLA_KV_RANK, MLA_ROPE,
    GDN_HEADS * GDN_DK, GDN_HEADS * GDN_DK,
    GDN_HEADS * GDN_DV, GDN_HEADS * GDN_DV,
    GDN_HEADS, GDN_HEADS,
    MOBA_HEADS * MOBA_DH, MOBA_HEADS * MOBA_DH, MOBA_HEADS * MOBA_DH,
    N_BRANCH * D_MODEL,
)
IN_COLS = sum(SPLIT_SIZES)
SPLIT_POINTS = tuple(sum(SPLIT_SIZES[:i + 1]) for i in range(len(SPLIT_SIZES) - 1))

kernel_name = 'hybrid_mla_gdn_moba_peer_adaln'


def rms_norm(x, gain=None):
    xf = x.astype(jnp.float32)
    y = xf * lax.rsqrt(jnp.mean(xf * xf, axis=-1, keepdims=True) + NORM_EPS)
    if gain is not None:
        y = y * gain.astype(jnp.float32)
    return y.astype(x.dtype)


def l2_norm(x):
    xf = x.astype(jnp.float32)
    return xf * lax.rsqrt(jnp.sum(xf * xf, axis=-1, keepdims=True) + NORM_EPS)


def modulate(x, shift, scale):
    return rms_norm(x) * (1.0 + scale[:, None, :]) + shift[:, None, :]


def apply_rope(x, positions):
    half = x.shape[-1] // 2
    inv_freq = ROPE_THETA ** (-jnp.arange(half, dtype=jnp.float32) / half)
    ang = positions.astype(jnp.float32)[..., None] * inv_freq
    cos, sin = jnp.cos(ang)[:, :, None, :], jnp.sin(ang)[:, :, None, :]
    xf = x.astype(jnp.float32)
    x1, x2 = xf[..., :half], xf[..., half:]
    return jnp.concatenate([x1 * cos - x2 * sin, x1 * sin + x2 * cos], axis=-1).astype(x.dtype)


def causal_attention(q, k, v, scale):
    B, S, H, Dq = q.shape
    nq = S // ATTN_Q_BLOCK
    qb = q.reshape(B, nq, ATTN_Q_BLOCK, H, Dq).transpose(1, 0, 2, 3, 4)
    kpos = jnp.arange(S)

    def one_block(args):
        qi, i = args
        s = jnp.einsum('bqhd,bkhd->bhqk', qi, k, preferred_element_type=jnp.float32) * scale
        qpos = i * ATTN_Q_BLOCK + jnp.arange(ATTN_Q_BLOCK)
        s = jnp.where(kpos[None, :] <= qpos[:, None], s, NEG_INF)
        p = jax.nn.softmax(s, axis=-1).astype(v.dtype)
        return jnp.einsum('bhqk,bkhd->bqhd', p, v)

    o = lax.map(one_block, (qb, jnp.arange(nq)))
    return o.transpose(1, 0, 2, 3, 4).reshape(B, S, H, v.shape[-1])


def mla_branch(cq, ckv, k_rope, positions, q_gain, w_uq, kv_gain, w_ukv, qn_gain, kn_gain):
    B, S, _ = cq.shape
    q = (rms_norm(cq, q_gain) @ w_uq).reshape(B, S, MLA_HEADS, MLA_QK)
    kv = (rms_norm(ckv, kv_gain) @ w_ukv).reshape(B, S, MLA_HEADS, MLA_NOPE + MLA_V)
    k_nope, v = kv[..., :MLA_NOPE], kv[..., MLA_NOPE:]
    k_pe = jnp.broadcast_to(k_rope[:, :, None, :], (B, S, MLA_HEADS, MLA_ROPE))
    k = jnp.concatenate([k_nope, k_pe], axis=-1)
    q = rms_norm(q, qn_gain)
    k = rms_norm(k, kn_gain)
    q = jnp.concatenate([q[..., :MLA_NOPE], apply_rope(q[..., MLA_NOPE:], positions)], axis=-1)
    k = jnp.concatenate([k[..., :MLA_NOPE], apply_rope(k[..., MLA_NOPE:], positions)], axis=-1)
    o = causal_attention(q, k, v, MLA_QK ** -0.5)
    return o.reshape(B, S, MLA_HEADS * MLA_V)


def causal_conv(x, w):
    C = x.shape[-1]
    return lax.conv_general_dilated(
        x, w[:, None, :].astype(x.dtype), window_strides=(1,), padding=[(GDN_CONV - 1, 0)],
        dimension_numbers=('NWC', 'WIO', 'NWC'), feature_group_count=C)


def gated_delta_rule_chunked(q, k, v, g, beta):
    B, S, H, DK = q.shape
    DV = v.shape[-1]
    C = GDN_CHUNK
    N = S // C
    f32 = jnp.float32

    def chunks(t):
        return t.astype(f32).reshape(B, N, C, H, -1).transpose(0, 3, 1, 2, 4)

    q, k, v = chunks(q), chunks(k), chunks(v)
    g = chunks(g[..., None])[..., 0]
    beta = chunks(beta[..., None])[..., 0]
    gc = jnp.cumsum(g, axis=-1)
    tri = jnp.tril(jnp.ones((C, C), dtype=bool))
    strict = jnp.tril(jnp.ones((C, C), dtype=bool), -1)
    diff = gc[..., :, None] - gc[..., None, :]
    decay = jnp.where(tri, jnp.exp(jnp.where(tri, diff, 0.0)), 0.0)
    k_beta = k * beta[..., None]
    a_mat = jnp.where(strict, jnp.einsum('bhnid,bhnjd->bhnij', k_beta, k) * decay, 0.0) + jnp.eye(C, dtype=f32)
    rhs = jnp.concatenate([v * beta[..., None], k_beta * jnp.exp(gc)[..., None]], axis=-1)
    sol = lax.linalg.triangular_solve(a_mat, rhs, left_side=True, lower=True, unit_diagonal=True)
    u, w = sol[..., :DV], sol[..., DV:]
    attn = jnp.einsum('bhnid,bhnjd->bhnij', q, k) * decay

    def step(state, xs):
        q_i, k_i, u_i, w_i, attn_i, gc_i = xs
        v_new = u_i - jnp.einsum('bhcd,bhde->bhce', w_i, state)
        o_i = (jnp.einsum('bhcd,bhde->bhce', q_i * jnp.exp(gc_i)[..., None], state)
               + jnp.einsum('bhij,bhje->bhie', attn_i, v_new))
        g_last = gc_i[..., -1:]
        state = (state * jnp.exp(g_last)[..., None]
                 + jnp.einsum('bhcd,bhce->bhde', k_i * jnp.exp(g_last - gc_i)[..., None], v_new))
        return state, o_i

    xs = tuple(jnp.moveaxis(t, 2, 0) for t in (q, k, u, w, attn, gc))
    state0 = jnp.zeros((B, H, DK, DV), f32)
    _, o = lax.scan(step, state0, xs)
    return o.transpose(1, 0, 3, 2, 4).reshape(B, S, H, DV)


def gdn_branch(q, k, v, z, a, b, conv_w, a_log, dt_bias, o_gain):
    B, S, _ = q.shape
    f32 = jnp.float32
    qkv = jax.nn.silu(causal_conv(jnp.concatenate([q, k, v], axis=-1), conv_w))
    qd, kd = GDN_HEADS * GDN_DK, 2 * GDN_HEADS * GDN_DK
    q = l2_norm(qkv[..., :qd].reshape(B, S, GDN_HEADS, GDN_DK)) * (GDN_DK ** -0.5)
    k = l2_norm(qkv[..., qd:kd].reshape(B, S, GDN_HEADS, GDN_DK))
    v = qkv[..., kd:].reshape(B, S, GDN_HEADS, GDN_DV)
    beta = jax.nn.sigmoid(b.astype(f32))
    g = -jnp.exp(a_log.astype(f32)) * jax.nn.softplus(a.astype(f32) + dt_bias.astype(f32))
    o = gated_delta_rule_chunked(q, k, v, g, beta).astype(z.dtype)
    o = rms_norm(o, o_gain) * jax.nn.silu(z.reshape(B, S, GDN_HEADS, GDN_DV))
    return o.reshape(B, S, GDN_HEADS * GDN_DV)


def moba_branch(q, k, v, qn_gain, kn_gain):
    B, S, _ = q.shape
    H, Dh, L = MOBA_HEADS, MOBA_DH, MOBA_BLOCK
    nb = -(-S // L)
    s_pad = nb * L
    topk = min(MOBA_TOPK, nb)
    f32 = jnp.float32
    q = rms_norm(q.reshape(B, S, H, Dh), qn_gain).transpose(0, 2, 1, 3)
    k = rms_norm(k.reshape(B, S, H, Dh), kn_gain).transpose(0, 2, 1, 3)
    v = v.reshape(B, S, H, Dh).transpose(0, 2, 1, 3)
    pad = ((0, 0), (0, 0), (0, s_pad - S), (0, 0))
    kb = jnp.pad(k, pad).reshape(B, H, nb, L, Dh)
    vb = jnp.pad(v, pad).reshape(B, H, nb, L, Dh)
    k_mean = jnp.mean(kb.astype(f32), axis=3)
    gate = jnp.einsum('bhsd,bhnd->bhsn', q.astype(f32), k_mean)
    q_blk = jnp.arange(S) // L
    gate = jnp.where(jnp.arange(nb)[None, :] < q_blk[:, None], gate, NEG_INF)
    _, sel = lax.top_k(gate, topk)
    valid = jnp.arange(topk)[None, :] < q_blk[:, None]
    scale = Dh ** -0.5
    gather = jax.vmap(jax.vmap(lambda blocks, idx: blocks[idx]))

    def one_chunk(ci):
        start = ci * MOBA_Q_CHUNK
        qc = lax.dynamic_slice_in_dim(q, start, MOBA_Q_CHUNK, axis=2)
        sc = lax.dynamic_slice_in_dim(sel, start, MOBA_Q_CHUNK, axis=2)
        vmask = lax.dynamic_slice_in_dim(valid, start, MOBA_Q_CHUNK, axis=0)
        k_sel = gather(kb, sc)
        v_sel = gather(vb, sc)
        s_sel = jnp.einsum('bhqd,bhqjld->bhqjl', qc, k_sel, preferred_element_type=f32) * scale
        s_sel = jnp.where(vmask[None, None, :, :, None], s_sel, NEG_INF)
        own = start // L
        k_own = lax.dynamic_index_in_dim(kb, own, axis=2, keepdims=False)
        v_own = lax.dynamic_index_in_dim(vb, own, axis=2, keepdims=False)
        s_own = jnp.einsum('bhqd,bhld->bhql', qc, k_own, preferred_element_type=f32) * scale
        qpos = start + jnp.arange(MOBA_Q_CHUNK)
        kpos = own * L + jnp.arange(L)
        s_own = jnp.where(kpos[None, :] <= qpos[:, None], s_own, NEG_INF)
        s = jnp.concatenate([s_sel.reshape(B, H, MOBA_Q_CHUNK, topk * L), s_own], axis=-1)
        p = jax.nn.softmax(s, axis=-1).astype(v.dtype)
        p_sel = p[..., :topk * L].reshape(B, H, MOBA_Q_CHUNK, topk, L)
        return (jnp.einsum('bhqjl,bhqjld->bhqd', p_sel, v_sel)
                + jnp.einsum('bhql,bhld->bhqd', p[..., topk * L:], v_own))

    o = lax.map(one_chunk, jnp.arange(S // MOBA_Q_CHUNK))
    return o.transpose(1, 0, 3, 2, 4).reshape(B, S, H * Dh)


def peer_ffn(h, w_query, sub_keys, expert_u, expert_v):
    B, S, D = h.shape
    T = B * S
    K = PEER_TOPK
    ht = h.reshape(T, D)
    qry = (ht @ w_query).reshape(T, PEER_HEADS, 2, PEER_DQ // 2)
    s_half = jnp.einsum('thpd,hpnd->thpn', qry, sub_keys, preferred_element_type=jnp.float32)
    v_half, i_half = lax.top_k(s_half, K)
    cand = v_half[:, :, 0, :, None] + v_half[:, :, 1, None, :]
    cand_idx = i_half[:, :, 0, :, None] * PEER_NKEYS + i_half[:, :, 1, None, :]
    top_s, top_pos = lax.top_k(cand.reshape(T, PEER_HEADS, K * K), K)
    experts = jnp.take_along_axis(cand_idx.reshape(T, PEER_HEADS, K * K), top_pos, axis=-1)
    gates = jax.nn.softmax(top_s, axis=-1)
    nchunk = T // PEER_TOK_CHUNK

    def one_chunk(args):
        x_c, e_c, g_c = args
        u = expert_u[e_c]
        act = jax.nn.gelu(jnp.einsum('td,thkd->thk', x_c, u), approximate=False)
        wgt = (g_c * act.astype(jnp.float32)).astype(x_c.dtype)
        return jnp.einsum('thk,thkd->td', wgt, expert_v[e_c])

    out = lax.map(one_chunk, (ht.reshape(nchunk, PEER_TOK_CHUNK, D),
                              experts.reshape(nchunk, PEER_TOK_CHUNK, PEER_HEADS, K),
                              gates.reshape(nchunk, PEER_TOK_CHUNK, PEER_HEADS, K)))
    return out.reshape(B, S, D)


def setup_inputs(seed: int = 0) -> dict:
    key = jax.random.key(seed)
    ks = jax.random.split(key, 24)
    f32 = jnp.float32

    def nrm(k, shape, scale):
        return jax.random.normal(k, shape, f32) * scale

    def gain(k, shape):
        return 1.0 + 0.02 * jax.random.normal(k, shape, f32)

    dt = jnp.exp(jax.random.uniform(ks[13], (DEPTH, GDN_HEADS), f32, minval=jnp.log(1e-3), maxval=jnp.log(1e-1)))
    positions = (jax.random.randint(ks[2], (BATCH, 1), 0, 1024, dtype=jnp.int32)
                 + jnp.arange(SEQ, dtype=jnp.int32)[None, :])
    return {
        'x': nrm(ks[0], (BATCH, SEQ, D_MODEL), 1.0),
        'c': nrm(ks[1], (BATCH, D_MODEL), 1.0),
        'positions': positions,
        'w_mod': nrm(ks[3], (DEPTH, D_MODEL, 6 * D_MODEL), 0.5 * D_MODEL ** -0.5),
        'b_mod': nrm(ks[4], (DEPTH, 6 * D_MODEL), 0.02),
        'w_in': nrm(ks[5], (DEPTH, D_MODEL, IN_COLS), D_MODEL ** -0.5),
        'mla_q_gain': gain(ks[6], (DEPTH, MLA_Q_RANK)),
        'mla_w_uq': nrm(ks[7], (DEPTH, MLA_Q_RANK, MLA_HEADS * MLA_QK), MLA_Q_RANK ** -0.5),
        'mla_kv_gain': gain(ks[8], (DEPTH, MLA_KV_RANK)),
        'mla_w_ukv': nrm(ks[9], (DEPTH, MLA_KV_RANK, MLA_HEADS * (MLA_NOPE + MLA_V)), MLA_KV_RANK ** -0.5),
        'mla_qn_gain': gain(ks[10], (DEPTH, MLA_QK)),
        'mla_kn_gain': gain(ks[11], (DEPTH, MLA_QK)),
        'gdn_conv_w': nrm(ks[12], (DEPTH, GDN_CONV, GDN_CONV_CH), GDN_CONV ** -0.5),
        'gdn_a_log': jnp.log(jax.random.uniform(ks[14], (DEPTH, GDN_HEADS), f32, minval=1.0, maxval=16.0)),
        'gdn_dt_bias': dt + jnp.log(-jnp.expm1(-dt)),
        'gdn_o_gain': gain(ks[15], (DEPTH, GDN_DV)),
        'moba_qn_gain': gain(ks[16], (DEPTH, MOBA_DH)),
        'moba_kn_gain': gain(ks[17], (DEPTH, MOBA_DH)),
        'w_branch': nrm(ks[18], (DEPTH, N_BRANCH, BRANCH_W, D_MODEL), BRANCH_W ** -0.5),
        'w_out': nrm(ks[19], (DEPTH, D_MODEL, D_MODEL), D_MODEL ** -0.5),
        'peer_w_query': nrm(ks[20], (DEPTH, D_MODEL, PEER_HEADS * PEER_DQ), D_MODEL ** -0.5),
        'peer_sub_keys': nrm(ks[21], (DEPTH, PEER_HEADS, 2, PEER_NKEYS, PEER_DQ // 2), (PEER_DQ // 2) ** -0.5),
        'peer_u': nrm(ks[22], (DEPTH, PEER_EXPERTS, D_MODEL), D_MODEL ** -0.5),
        'peer_v': nrm(ks[23], (DEPTH, PEER_EXPERTS, D_MODEL), PEER_HEADS ** -0.5),
    }


def reference(x, c, positions, w_mod, b_mod, w_in, mla_q_gain, mla_w_uq, mla_kv_gain, mla_w_ukv,
              mla_qn_gain, mla_kn_gain, gdn_conv_w, gdn_a_log, gdn_dt_bias, gdn_o_gain,
              moba_qn_gain, moba_kn_gain, w_branch, w_out, peer_w_query, peer_sub_keys,
              peer_u, peer_v):
    B, S, D = x.shape
    c_act = jax.nn.silu(c)
    for l in range(DEPTH):
        mod = c_act @ w_mod[l] + b_mod[l]
        sh_a, sc_a, g_a, sh_f, sc_f, g_f = jnp.split(mod, 6, axis=-1)
        h = modulate(x, sh_a, sc_a)
        (cq, ckv, k_rope, gq, gk, gv, gz, ga, gb,
         mq, mk, mv, gate_logits) = jnp.split(h @ w_in[l], SPLIT_POINTS, axis=-1)
        o_mla = mla_branch(cq, ckv, k_rope, positions, mla_q_gain[l], mla_w_uq[l], mla_kv_gain[l],
                           mla_w_ukv[l], mla_qn_gain[l], mla_kn_gain[l])
        o_gdn = gdn_branch(gq, gk, gv, gz, ga, gb, gdn_conv_w[l], gdn_a_log[l], gdn_dt_bias[l], gdn_o_gain[l])
        o_moba = moba_branch(mq, mk, mv, moba_qn_gain[l], moba_kn_gain[l])
        branches = jnp.stack([o_mla, o_gdn, o_moba], axis=2)
        proj = jnp.einsum('bsnw,nwd->bsnd', branches, w_branch[l])
        gates = jax.nn.sigmoid(gate_logits.reshape(B, S, N_BRANCH, D))
        y = jnp.einsum('bsnd,bsnd->bsd', gates, proj) @ w_out[l]
        x = x + g_a[:, None, :] * y
        h2 = modulate(x, sh_f, sc_f)
        x = x + g_f[:, None, :] * peer_ffn(h2, peer_w_query[l], peer_sub_keys[l], peer_u[l], peer_v[l])
    return x
```

```python
import functools

import jax
import jax.numpy as jnp
from jax import lax
from jax.experimental import pallas as pl
from jax.experimental.pallas import tpu as pltpu

F32 = jnp.float32
BF16 = jnp.bfloat16
HI = lax.Precision.HIGHEST

D_MODEL = 1024
MLA_HEADS = 8
MLA_Q_RANK = 256
MLA_KV_RANK = 128
MLA_NOPE = 64
MLA_ROPE = 32
MLA_V = 64
MLA_QK = MLA_NOPE + MLA_ROPE
ROPE_THETA = 10000.0
HEADS = 8
HEAD_PAD = 128
GDN_DK = 64
GDN_DV = 64
GDN_CONV = 4
GDN_CHUNK = 64
MOBA_DH = 64
MOBA_BLOCK = 256
MOBA_TOPK = 3
N_BRANCH = 3
BRANCH_W = 512
PEER_HEADS = 8
PEER_NKEYS = 128
PEER_TOPK = 16
PEER_SEL = PEER_HEADS * PEER_TOPK
NORM_EPS = 1e-6
NEG_INF = -1e30

COL_LAT = 0
COL_GQ = 512
COL_MQ = 2560
COL_GATE = 4096
IN_COLS_PAD = 7168

VMEM_LIMIT = 56 * 1024 * 1024


def _cp(sem, vmem=None):
    return pltpu.CompilerParams(dimension_semantics=sem, vmem_limit_bytes=vmem or VMEM_LIMIT)


def _nt(a, b, precision=None):
    return lax.dot_general(a, b, (((1,), (1,)), ((), ())), precision=precision, preferred_element_type=F32)


def _tn(a, b, precision=None):
    return lax.dot_general(a, b, (((0,), (0,)), ((), ())), precision=precision, preferred_element_type=F32)


def _mm(a, b, precision=None):
    return jnp.dot(a, b, precision=precision, preferred_element_type=F32)


def _silu(x):
    return x * jax.nn.sigmoid(x)


def _mod_kernel(c_ref, w_ref, b_ref, o_ref):
    c = c_ref[...]
    o_ref[0] = _mm(_silu(c), w_ref[0], HI) + b_ref[0]


def _mod_all(c, w_mod, b_mod):
    depth, d, n = w_mod.shape
    bsz = c.shape[0]
    tn = 1024
    return pl.pallas_call(
        _mod_kernel,
        grid=(depth, n // tn),
        in_specs=[pl.BlockSpec((bsz, d), lambda l, j: (0, 0)),
                  pl.BlockSpec((1, d, tn), lambda l, j: (l, 0, j)),
                  pl.BlockSpec((1, 1, tn), lambda l, j: (l, 0, j))],
        out_specs=pl.BlockSpec((1, bsz, tn), lambda l, j: (l, 0, j)),
        out_shape=jax.ShapeDtypeStruct((depth, bsz, n), F32),
        compiler_params=_cp(("arbitrary", "arbitrary")),
        name="adaln_mod",
    )(c, w_mod, b_mod.reshape(depth, 1, n))


def _modmm_kernel(x_ref, sh_ref, sc_ref, w_ref, o_ref, h_ref, hb_scr):
    @pl.when(pl.program_id(1) == 0)
    def _():
        x = x_ref[...]
        h = x * lax.rsqrt(jnp.mean(x * x, axis=-1, keepdims=True) + NORM_EPS)
        h = h * (1.0 + sc_ref[0]) + sh_ref[0]
        h_ref[...] = h
        hb_scr[...] = h.astype(BF16)

    o_ref[...] = _mm(hb_scr[...], w_ref[...])


def _modulate_matmul(x, shift, scale, w_bf16, seq, tm=512, tn=1024):
    t, d = x.shape
    n = w_bf16.shape[1]
    bsz = shift.shape[0]
    bidx = lambda i, j: ((i * tm) // seq, 0, 0)
    return pl.pallas_call(
        _modmm_kernel,
        grid=(t // tm, n // tn),
        in_specs=[pl.BlockSpec((tm, d), lambda i, j: (i, 0)),
                  pl.BlockSpec((1, 1, d), bidx),
                  pl.BlockSpec((1, 1, d), bidx),
                  pl.BlockSpec((d, tn), lambda i, j: (0, j))],
        out_specs=[pl.BlockSpec((tm, tn), lambda i, j: (i, j)),
                   pl.BlockSpec((tm, d), lambda i, j: (i, 0))],
        out_shape=[jax.ShapeDtypeStruct((t, n), F32), jax.ShapeDtypeStruct((t, d), F32)],
        scratch_shapes=[pltpu.VMEM((tm, d), BF16)],
        compiler_params=_cp(("arbitrary", "arbitrary")),
        name="modulate_matmul",
    )(x, shift.reshape(bsz, 1, d), scale.reshape(bsz, 1, d), w_bf16)


def _mla_prep_kernel(p_ref, pos_ref, qg_ref, wuq_ref, kvg_ref, wuk_ref, wuv_ref, qng_ref, kng_ref, invf_ref,
                     q_out, k_out, v_out):
    tm = p_ref.shape[0]
    cq = p_ref[:, 0:MLA_Q_RANK]
    ckv = p_ref[:, MLA_Q_RANK:MLA_Q_RANK + MLA_KV_RANK]
    misc = p_ref[:, MLA_Q_RANK + MLA_KV_RANK:MLA_Q_RANK + MLA_KV_RANK + 128]

    def rms(v, n):
        return v * lax.rsqrt(jnp.sum(v * v, axis=-1, keepdims=True) * (1.0 / n) + NORM_EPS)

    qn = (rms(cq, MLA_Q_RANK) * qg_ref[...]).astype(BF16)
    q_all = _mm(qn, wuq_ref[...])
    kvn = (rms(ckv, MLA_KV_RANK) * kvg_ref[...]).astype(BF16)
    k_all = _mm(kvn, wuk_ref[...])
    v_out[...] = _mm(kvn, wuv_ref[...]).astype(BF16)

    lane = lax.broadcasted_iota(jnp.int32, (tm, HEAD_PAD), 1)
    in_rope = (lane >= MLA_NOPE) & (lane < MLA_QK)
    k_rope = jnp.where(in_rope, pltpu.roll(misc, MLA_NOPE, 1), 0.0)
    ang = pos_ref[...].astype(F32) * invf_ref[...]
    cos = jnp.cos(ang)
    sin = jnp.sin(ang)
    half = MLA_ROPE // 2
    c_tab = jnp.where(lane < MLA_NOPE, 1.0, jnp.where(in_rope, cos, 0.0))
    s_lo = jnp.where(in_rope & (lane < MLA_NOPE + half), -sin, 0.0)
    s_hi = jnp.where(in_rope & (lane >= MLA_NOPE + half), sin, 0.0)

    def finish(xh, gain):
        xh = xh * lax.rsqrt(jnp.sum(xh * xh, axis=-1, keepdims=True) * (1.0 / MLA_QK) + NORM_EPS) * gain
        return xh * c_tab + pltpu.roll(xh, HEAD_PAD - half, 1) * s_lo + pltpu.roll(xh, half, 1) * s_hi

    scale = MLA_QK ** -0.5
    for h in range(MLA_HEADS):
        sl = slice(h * HEAD_PAD, (h + 1) * HEAD_PAD)
        q_out[:, sl] = (finish(q_all[:, sl], qng_ref[...]) * scale).astype(BF16)
        k_out[:, sl] = finish(k_all[:, sl] + k_rope, kng_ref[...]).astype(BF16)


def _mla_prep(proj, pos, lw, tm=256):
    t = proj.shape[0]
    full = lambda shp: pl.BlockSpec(shp, lambda i: (0,) * len(shp))
    return pl.pallas_call(
        _mla_prep_kernel,
        grid=(t // tm,),
        in_specs=[pl.BlockSpec((tm, 512), lambda i: (i, 0)),
                  pl.BlockSpec((tm, 1), lambda i: (i, 0)),
                  full((1, MLA_Q_RANK)), full((MLA_Q_RANK, HEADS * HEAD_PAD)),
                  full((1, MLA_KV_RANK)), full((MLA_KV_RANK, HEADS * HEAD_PAD)),
                  full((MLA_KV_RANK, HEADS * MLA_V)),
                  full((1, HEAD_PAD)), full((1, HEAD_PAD)), full((1, HEAD_PAD))],
        out_specs=[pl.BlockSpec((tm, HEADS * HEAD_PAD), lambda i: (i, 0)),
                   pl.BlockSpec((tm, HEADS * HEAD_PAD), lambda i: (i, 0)),
                   pl.BlockSpec((tm, HEADS * MLA_V), lambda i: (i, 0))],
        out_shape=[jax.ShapeDtypeStruct((t, HEADS * HEAD_PAD), BF16),
                   jax.ShapeDtypeStruct((t, HEADS * HEAD_PAD), BF16),
                   jax.ShapeDtypeStruct((t, HEADS * MLA_V), BF16)],
        compiler_params=_cp(("arbitrary",)),
        name="mla_prep",
    )(proj, pos, lw["mla_q_gain"], lw["mla_w_uq"], lw["mla_kv_gain"], lw["mla_w_uk"], lw["mla_w_uv"],
      lw["mla_qn_gain"], lw["mla_kn_gain"], lw["rope_inv_freq"])


def _attn_kernel(q_ref, k_ref, v_ref, o_ref, *, tq):
    i = pl.program_id(2)
    row = lax.broadcasted_iota(jnp.int32, (tq, tq), 0)
    col = lax.broadcasted_iota(jnp.int32, (tq, tq), 1)
    outs = []
    for hh in range(2):
        q = q_ref[:, hh * HEAD_PAD:(hh + 1) * HEAD_PAD]

        def step(j, carry, masked, hh=hh, q=q):
            m, l, acc = carry
            start = pl.multiple_of(j * tq, tq)
            kk = k_ref[pl.ds(start, tq), hh * HEAD_PAD:(hh + 1) * HEAD_PAD]
            vv = v_ref[pl.ds(start, tq), :]
            s = _nt(q, kk)
            if masked:
                s = jnp.where(col <= row, s, NEG_INF)
            m_new = jnp.maximum(m, jnp.max(s, axis=-1, keepdims=True))
            p = jnp.exp(s - m_new)
            alpha = jnp.exp(m - m_new)
            l = alpha * l + jnp.sum(p, axis=-1, keepdims=True)
            acc = alpha * acc + _mm(p.astype(BF16), vv)
            return m_new, l, acc

        init = (jnp.full((tq, 1), -jnp.inf, F32), jnp.zeros((tq, 1), F32), jnp.zeros((tq, 2 * MLA_V), F32))
        carry = lax.fori_loop(0, i, functools.partial(step, masked=False), init)
        m, l, acc = step(i, carry, True)
        outs.append(acc / l)
    lane = lax.broadcasted_iota(jnp.int32, (tq, 2 * MLA_V), 1)
    o_ref[...] = jnp.where(lane < MLA_V, outs[0], outs[1])


def _causal_attention(q, k, v, bsz, seq, tq=512):
    t = q.shape[0]
    nq = seq // tq
    return pl.pallas_call(
        functools.partial(_attn_kernel, tq=tq),
        grid=(bsz, HEADS // 2, nq),
        in_specs=[pl.BlockSpec((tq, 2 * HEAD_PAD), lambda b, hp, i: (b * nq + i, hp)),
                  pl.BlockSpec((seq, 2 * HEAD_PAD), lambda b, hp, i: (b, hp)),
                  pl.BlockSpec((seq, 2 * MLA_V), lambda b, hp, i: (b, hp))],
        out_specs=pl.BlockSpec((tq, 2 * MLA_V), lambda b, hp, i: (b * nq + i, hp)),
        out_shape=jax.ShapeDtypeStruct((t, HEADS * MLA_V), F32),
        compiler_params=_cp(("arbitrary", "arbitrary", "arbitrary")),
        name="causal_attention",
    )(q, k, v)


def _moba_prep_kernel(mq_ref, mk_ref, mv_ref, qg_ref, kg_ref, e_ref, et_ref, q_out, k_out, v_out, kmean_scr, *, nb):
    tm = mq_ref.shape[0]
    n = pl.program_id(0) % nb

    @pl.when(n == 0)
    def _():
        kmean_scr[...] = jnp.zeros_like(kmean_scr)

    def headnorm(x, gain):
        ss = _mm(x * x, e_ref[...], HI)
        inv = lax.rsqrt(ss * (1.0 / MOBA_DH) + NORM_EPS)
        return x * _mm(inv, et_ref[...], HI) * gain

    qn = headnorm(mq_ref[...], qg_ref[...])
    kn = headnorm(mk_ref[...], kg_ref[...])
    v_out[...] = mv_ref[...].astype(BF16)
    kmean_scr[pl.ds(n, 1), :] = jnp.mean(kn, axis=0, keepdims=True)
    km = kmean_scr[...]

    lane = lax.broadcasted_iota(jnp.int32, (tm, nb), 1)
    zpad = jnp.zeros((tm, HEAD_PAD - MOBA_DH - nb), F32)
    onehot = jnp.where(lane == n, 1.0, 0.0)
    for h in range(HEADS):
        sl = slice(h * MOBA_DH, (h + 1) * MOBA_DH)
        gate = _nt(qn[:, sl], km[:, sl], HI)
        gate = jnp.where(lane < n, gate, -jnp.inf)
        pen = jnp.full((tm, nb), NEG_INF, F32)
        for r in range(MOBA_TOPK):
            mx = jnp.max(gate, axis=-1, keepdims=True)
            idx = jnp.min(jnp.where(gate == mx, lane, nb), axis=-1, keepdims=True)
            hit = (lane == idx) & (r < n)
            pen = jnp.where(hit, 0.0, pen)
            gate = jnp.where(lane == idx, -jnp.inf, gate)
        pen = jnp.where(lane == n, 0.0, pen)
        osl = slice(h * HEAD_PAD, (h + 1) * HEAD_PAD)
        q_out[:, osl] = jnp.concatenate([qn[:, sl] * (MOBA_DH ** -0.5), pen, zpad], axis=1).astype(BF16)
        k_out[:, osl] = jnp.concatenate([kn[:, sl], onehot, zpad], axis=1).astype(BF16)


def _moba_prep(proj, lw, seq):
    t = proj.shape[0]
    tm = MOBA_BLOCK
    nb = seq // tm
    full = lambda shp: pl.BlockSpec(shp, lambda i: (0,) * len(shp))
    c0 = COL_MQ // 512
    return pl.pallas_call(
        functools.partial(_moba_prep_kernel, nb=nb),
        grid=(t // tm,),
        in_specs=[pl.BlockSpec((tm, 512), lambda i: (i, c0)),
                  pl.BlockSpec((tm, 512), lambda i: (i, c0 + 1)),
                  pl.BlockSpec((tm, 512), lambda i: (i, c0 + 2)),
                  full((1, 512)), full((1, 512)), full((512, HEADS)), full((HEADS, 512))],
        out_specs=[pl.BlockSpec((tm, HEADS * HEAD_PAD), lambda i: (i, 0)),
                   pl.BlockSpec((tm, HEADS * HEAD_PAD), lambda i: (i, 0)),
                   pl.BlockSpec((tm, HEADS * MOBA_DH), lambda i: (i, 0))],
        out_shape=[jax.ShapeDtypeStruct((t, HEADS * HEAD_PAD), BF16),
                   jax.ShapeDtypeStruct((t, HEADS * HEAD_PAD), BF16),
                   jax.ShapeDtypeStruct((t, HEADS * MOBA_DH), BF16)],
        scratch_shapes=[pltpu.VMEM((nb, HEADS * MOBA_DH), F32)],
        compiler_params=_cp(("arbitrary",)),
        name="moba_prep",
    )(proj, proj, proj, lw["moba_qn_gain"], lw["moba_kn_gain"], lw["head_sum"], lw["head_expand"])


def _gdn_prep_kernel(gq_ref, gk_ref, gv_ref, misc_ref, wq_ref, wk_ref, wv_ref, alog_ref, dtb_ref, e_ref, et_ref,
                     q_out, k_out, v_out, gb_out, ext_scr, *, tiles_per_seq):
    tm = gq_ref.shape[0]
    pad = 8

    @pl.when(pl.program_id(0) % tiles_per_seq == 0)
    def _():
        ext_scr[:, 0:pad, :] = jnp.zeros((3, pad, ext_scr.shape[2]), F32)

    def conv_silu(s, x_ref, w_ref):
        ext_scr[s, pad:pad + tm, :] = x_ref[...]
        y = jnp.zeros(x_ref.shape, F32)
        for i in range(GDN_CONV):
            y = y + ext_scr[s, pl.ds(pad - (GDN_CONV - 1) + i, tm), :] * w_ref[i:i + 1, :]
        ext_scr[s, 0:pad, :] = ext_scr[s, tm:tm + pad, :]
        return _silu(y)

    def l2n(x):
        ss = _mm(x * x, e_ref[...], HI)
        return x * _mm(lax.rsqrt(ss + NORM_EPS), et_ref[...], HI)

    q_out[...] = l2n(conv_silu(0, gq_ref, wq_ref)) * (GDN_DK ** -0.5)
    k_out[...] = l2n(conv_silu(1, gk_ref, wk_ref))
    v_out[...] = conv_silu(2, gv_ref, wv_ref)
    misc = misc_ref[...]
    a = misc[:, MLA_ROPE:MLA_ROPE + HEADS]
    b = misc[:, MLA_ROPE + HEADS:MLA_ROPE + 2 * HEADS]
    g = -jnp.exp(alog_ref[...]) * jax.nn.softplus(a + dtb_ref[...])
    beta = jax.nn.sigmoid(b)
    gb_out[...] = jnp.concatenate([g, beta, jnp.zeros((tm, 128 - 2 * HEADS), F32)], axis=1)


def _gdn_prep(proj, lw, seq, tm=256):
    t = proj.shape[0]
    full = lambda shp: pl.BlockSpec(shp, lambda i: (0,) * len(shp))
    c0 = COL_GQ // 512
    tok = lambda w: pl.BlockSpec((tm, w), lambda i: (i, 0))
    return pl.pallas_call(
        functools.partial(_gdn_prep_kernel, tiles_per_seq=seq // tm),
        grid=(t // tm,),
        in_specs=[pl.BlockSpec((tm, 512), lambda i: (i, c0)),
                  pl.BlockSpec((tm, 512), lambda i: (i, c0 + 1)),
                  pl.BlockSpec((tm, 512), lambda i: (i, c0 + 2)),
                  pl.BlockSpec((tm, 128), lambda i: (i, 3)),
                  full((GDN_CONV, 512)), full((GDN_CONV, 512)), full((GDN_CONV, 512)),
                  full((1, HEADS)), full((1, HEADS)), full((512, HEADS)), full((HEADS, 512))],
        out_specs=[tok(512), tok(512), tok(512), tok(128)],
        out_shape=[jax.ShapeDtypeStruct((t, 512), F32)] * 3 + [jax.ShapeDtypeStruct((t, 128), F32)],
        scratch_shapes=[pltpu.VMEM((3, tm + 8, 512), F32)],
        compiler_params=_cp(("arbitrary",)),
        name="gdn_prep",
    )(proj, proj, proj, proj, lw["conv_wq"], lw["conv_wk"], lw["conv_wv"], lw["gdn_a_log"], lw["gdn_dt_bias"],
      lw["head_sum"], lw["head_expand"])


def _gdn_scan_kernel(q_ref, k_ref, v_ref, gb_ref, z_ref, og_ref, o_ref, s_scr, *, chunks):
    c = GDN_CHUNK

    @pl.when(pl.program_id(1) == 0)
    def _():
        s_scr[...] = jnp.zeros_like(s_scr)

    ri = lax.broadcasted_iota(jnp.int32, (c, c), 0)
    ci = lax.broadcasted_iota(jnp.int32, (c, c), 1)
    tri = ci <= ri
    strict = ci < ri
    ltri = jnp.where(tri, 1.0, 0.0)
    for cc in range(chunks):
        rows = slice(cc * c, (cc + 1) * c)
        gb = gb_ref[rows, :]
        gc8 = _mm(ltri, gb[:, 0:HEADS], HI)
        gc8_t = gc8.T
        beta8 = gb[:, HEADS:2 * HEADS]
        for h in range(HEADS):
            sl = slice(h * GDN_DK, (h + 1) * GDN_DK)
            q = q_ref[rows, sl]
            k = k_ref[rows, sl]
            v = v_ref[rows, sl]
            gcol = gc8[:, h:h + 1]
            grow = gc8_t[h:h + 1, :]
            decay = jnp.where(tri, jnp.exp(jnp.where(tri, gcol - grow, 0.0)), 0.0)
            bcol = beta8[:, h:h + 1]
            kb = k * bcol
            kb16 = kb.astype(BF16)
            k16 = k.astype(BF16)
            a_neg = jnp.where(strict, -(_nt(kb16, k16) * decay), 0.0)
            x = jnp.concatenate([v * bcol, kb * jnp.exp(gcol)], axis=1)
            for lvl in range(6):
                a16 = a_neg.astype(BF16)
                if lvl < 5:
                    r = _mm(a16, jnp.concatenate([a_neg, x], axis=1).astype(BF16))
                    a_neg = r[:, 0:c]
                    x = x + r[:, c:]
                else:
                    x = x + _mm(a16, x.astype(BF16))
            u = x[:, 0:GDN_DV]
            w = x[:, GDN_DV:]
            attn = _nt(q.astype(BF16), k16) * decay
            s = s_scr[h]
            s16 = s.astype(BF16)
            v_new = u - _mm(w.astype(BF16), s16)
            v_new16 = v_new.astype(BF16)
            o = _mm((q * jnp.exp(gcol)).astype(BF16), s16) + _mm(attn.astype(BF16), v_new16)
            g_last = gc8[c - 1:c, h:h + 1]
            s_scr[h] = s * jnp.exp(g_last) + _tn((k * jnp.exp(g_last - gcol)).astype(BF16), v_new16)
            o = o * lax.rsqrt(jnp.mean(o * o, axis=-1, keepdims=True) + NORM_EPS) * og_ref[...]
            o_ref[rows, sl] = o * _silu(z_ref[rows, sl])


def _gdn_scan(qn, kn, vv, gb, proj, lw, bsz, seq, chunks=4):
    t = qn.shape[0]
    ct = chunks * GDN_CHUNK
    ns = seq // ct
    tok = lambda w: pl.BlockSpec((ct, w), lambda b, i: (b * ns + i, 0))
    return pl.pallas_call(
        functools.partial(_gdn_scan_kernel, chunks=chunks),
        grid=(bsz, ns),
        in_specs=[tok(512), tok(512), tok(512), tok(128),
                  pl.BlockSpec((ct, 512), lambda b, i: (b * ns + i, COL_GQ // 512 + 3)),
                  pl.BlockSpec((1, GDN_DV), lambda b, i: (0, 0))],
        out_specs=tok(512),
        out_shape=jax.ShapeDtypeStruct((t, 512), F32),
        scratch_shapes=[pltpu.VMEM((HEADS, GDN_DK, GDN_DV), F32)],
        compiler_params=_cp(("arbitrary", "arbitrary")),
        name="gdn_scan",
    )(qn, kn, vv, gb, proj, lw["gdn_o_gain"])


def _merge_kernel(oa_ref, ob_ref, oc_ref, g0_ref, g1_ref, g2_ref, x_ref, ga_ref, wb_ref, wo_ref, o_ref):
    y = None
    for o_n, g_n, n in ((oa_ref, g0_ref, 0), (ob_ref, g1_ref, 1), (oc_ref, g2_ref, 2)):
        term = jax.nn.sigmoid(g_n[...]) * _mm(o_n[...].astype(BF16), wb_ref[n])
        y = term if y is None else y + term
    o_ref[...] = x_ref[...] + ga_ref[0] * _mm(y.astype(BF16), wo_ref[...])


def _merge(o_mla, o_gdn, o_moba, proj, x, g_a, lw, seq, tm=512):
    t, d = x.shape
    bsz = g_a.shape[0]
    tok = lambda w: pl.BlockSpec((tm, w), lambda i: (i, 0))
    gcol = COL_GATE // d
    return pl.pallas_call(
        _merge_kernel,
        grid=(t // tm,),
        in_specs=[tok(BRANCH_W), tok(BRANCH_W), tok(BRANCH_W),
                  pl.BlockSpec((tm, d), lambda i: (i, gcol)),
                  pl.BlockSpec((tm, d), lambda i: (i, gcol + 1)),
                  pl.BlockSpec((tm, d), lambda i: (i, gcol + 2)),
                  tok(d),
                  pl.BlockSpec((1, 1, d), lambda i: ((i * tm) // seq, 0, 0)),
                  pl.BlockSpec((N_BRANCH, BRANCH_W, d), lambda i: (0, 0, 0)),
                  pl.BlockSpec((d, d), lambda i: (0, 0))],
        out_specs=tok(d),
        out_shape=jax.ShapeDtypeStruct((t, d), F32),
        compiler_params=_cp(("arbitrary",)),
        name="branch_merge",
    )(o_mla, o_gdn, o_moba, proj, proj, proj, x, g_a.reshape(bsz, 1, d), lw["w_branch"], lw["w_out"])


def _topk_rows(s, k):
    n = s.shape[0]
    ri = lax.broadcasted_iota(jnp.int32, s.shape, 0)
    vals, idxs = [], []
    for _ in range(k):
        mx = jnp.max(s, axis=0, keepdims=True)
        idx = jnp.min(jnp.where(s == mx, ri, n), axis=0, keepdims=True)
        vals.append(mx)
        idxs.append(idx)
        s = jnp.where(ri == idx, -jnp.inf, s)
    return jnp.concatenate(vals, axis=0), jnp.concatenate(idxs, axis=0)


def _peer_select_kernel(qry_ref, keys_ref, ids_out, gates_out):
    kk = PEER_TOPK
    ids_rows, gate_rows = [], []
    for h in range(PEER_HEADS):
        halves = []
        for p in range(2):
            g = h * 2 + p
            s_t = _nt(keys_ref[g], qry_ref[:, g * 128:(g + 1) * 128], HI)
            halves.append(_topk_rows(s_t, kk))
        (v1, i1), (v2, i2) = halves
        cand, a_of, b_of = [], [], []
        for a in range(kk):
            nb = kk // (a + 1)
            cand.append(v1[a:a + 1, :] + v2[0:nb, :])
            a_of += [a] * nb
            b_of += list(range(nb))
        cand = jnp.concatenate(cand, axis=0)
        top_s, top_r = _topk_rows(cand, kk)
        a_sel = jnp.zeros_like(top_r)
        b_sel = jnp.zeros_like(top_r)
        for r, (a, b) in enumerate(zip(a_of, b_of)):
            hit = top_r == r
            a_sel = jnp.where(hit, a, a_sel)
            b_sel = jnp.where(hit, b, b_sel)
        e1 = jnp.zeros_like(top_r)
        e2 = jnp.zeros_like(top_r)
        for a in range(kk):
            e1 = jnp.where(a_sel == a, i1[a:a + 1, :], e1)
            e2 = jnp.where(b_sel == a, i2[a:a + 1, :], e2)
        ids_rows.append(e1 * PEER_NKEYS + e2)
        ex = jnp.exp(top_s - top_s[0:1, :])
        gate_rows.append(ex / jnp.sum(ex, axis=0, keepdims=True))
    ids_out[...] = jnp.concatenate(ids_rows, axis=0).T
    gates_out[...] = jnp.concatenate(gate_rows, axis=0).T


def _peer_select(qry, sub_keys, tm=256):
    t = qry.shape[0]
    return pl.pallas_call(
        _peer_select_kernel,
        grid=(t // tm,),
        in_specs=[pl.BlockSpec((tm, 2 * PEER_HEADS * 128), lambda i: (i, 0)),
                  pl.BlockSpec((2 * PEER_HEADS, PEER_NKEYS, 128), lambda i: (0, 0, 0))],
        out_specs=[pl.BlockSpec((tm, PEER_SEL), lambda i: (i, 0)),
                   pl.BlockSpec((tm, PEER_SEL), lambda i: (i, 0))],
        out_shape=[jax.ShapeDtypeStruct((t, PEER_SEL), jnp.int32), jax.ShapeDtypeStruct((t, PEER_SEL), F32)],
        compiler_params=_cp(("arbitrary",)),
        name="peer_select",
    )(qry, sub_keys)


def _peer_apply_kernel(ids_ref, ids_next_ref, gates_ref, h_ref, x_ref, gf_ref, u_hbm, v_hbm, o_ref,
                       ubuf, vbuf, sems, *, tt):
    s = pl.program_id(0)
    ns = pl.num_programs(0)
    rows = tt * PEER_SEL

    def start_gather(idr, slot):
        def tok(t, carry):
            def one(e, carry2):
                r = t * PEER_SEL + e
                ex = idr[t, e]
                pltpu.make_async_copy(u_hbm.at[pl.ds(ex, 1)], ubuf.at[slot, pl.ds(r, 1)], sems.at[0, slot]).start()
                pltpu.make_async_copy(v_hbm.at[pl.ds(ex, 1)], vbuf.at[slot, pl.ds(r, 1)], sems.at[1, slot]).start()
                return carry2
            return lax.fori_loop(0, PEER_SEL, one, carry, unroll=8)
        lax.fori_loop(0, tt, tok, 0)

    def wait_gather(slot):
        pltpu.make_async_copy(u_hbm.at[pl.ds(0, rows)], ubuf.at[slot], sems.at[0, slot]).wait()
        pltpu.make_async_copy(v_hbm.at[pl.ds(0, rows)], vbuf.at[slot], sems.at[1, slot]).wait()

    slot = s % 2

    @pl.when(s == 0)
    def _():
        start_gather(ids_ref, 0)

    @pl.when(s + 1 < ns)
    def _():
        start_gather(ids_next_ref, 1 - slot)

    wait_gather(slot)
    d = h_ref.shape[1]
    hb = h_ref[...]
    ub = ubuf[slot].reshape(tt, PEER_SEL, d)
    act = jnp.sum(ub * hb[:, None, :], axis=-1)
    gel = 0.5 * act * (1.0 + lax.erf(act * (2.0 ** -0.5)))
    wgt = gates_ref[...] * gel
    vb = vbuf[slot].reshape(tt, PEER_SEL, d)
    out = jnp.sum(wgt[:, :, None] * vb, axis=1)
    o_ref[...] = x_ref[...] + gf_ref[0] * out


def _peer_apply(ids, gates, h2, x, g_f, u, v, seq, tt=16):
    t, d = x.shape
    bsz = g_f.shape[0]
    ns = t // tt
    tok = lambda w: pl.BlockSpec((tt, w), lambda i: (i, 0))
    return pl.pallas_call(
        functools.partial(_peer_apply_kernel, tt=tt),
        grid=(ns,),
        in_specs=[pl.BlockSpec((tt, PEER_SEL), lambda i: (i, 0), memory_space=pltpu.SMEM),
                  pl.BlockSpec((tt, PEER_SEL), lambda i: (jnp.minimum(i + 1, ns - 1), 0), memory_space=pltpu.SMEM),
                  tok(PEER_SEL), tok(d), tok(d),
                  pl.BlockSpec((1, 1, d), lambda i: ((i * tt) // seq, 0, 0)),
                  pl.BlockSpec(memory_space=pl.ANY),
                  pl.BlockSpec(memory_space=pl.ANY)],
        out_specs=tok(d),
        out_shape=jax.ShapeDtypeStruct((t, d), F32),
        scratch_shapes=[pltpu.VMEM((2, tt * PEER_SEL, d), F32),
                        pltpu.VMEM((2, tt * PEER_SEL, d), F32),
                        pltpu.SemaphoreType.DMA((2, 2))],
        compiler_params=_cp(("arbitrary",)),
        name="peer_apply",
    )(ids, ids, gates, h2, x, g_f.reshape(bsz, 1, d), u, v)


def _prep_layer_weights(l, w_in, mla_q_gain, mla_w_uq, mla_kv_gain, mla_w_ukv, mla_qn_gain, mla_kn_gain,
                        gdn_conv_w, gdn_a_log, gdn_dt_bias, gdn_o_gain, moba_qn_gain, moba_kn_gain,
                        w_branch, w_out, peer_w_query, peer_sub_keys):
    d = D_MODEL
    w = w_in[l]
    o = 0
    parts = {}
    for name, width in (("cq", 256), ("ckv", 128), ("kr", 32), ("gq", 512), ("gk", 512), ("gv", 512), ("gz", 512),
                        ("ga", 8), ("gb", 8), ("mq", 512), ("mk", 512), ("mv", 512), ("gate", 3072)):
        parts[name] = w[:, o:o + width]
        o += width
    w_in_p = jnp.concatenate(
        [parts["cq"], parts["ckv"], parts["kr"], parts["ga"], parts["gb"], jnp.zeros((d, 80), F32),
         parts["gq"], parts["gk"], parts["gv"], parts["gz"], parts["mq"], parts["mk"], parts["mv"], parts["gate"]],
        axis=1).astype(BF16)

    def pad_heads(m, width):
        r = m.shape[0]
        return jnp.pad(m.reshape(r, HEADS, width), ((0, 0), (0, 0), (0, HEAD_PAD - width))).reshape(r, HEADS * HEAD_PAD)

    ukv = mla_w_ukv[l].reshape(MLA_KV_RANK, HEADS, MLA_NOPE + MLA_V)
    pad1 = lambda g, n: jnp.pad(g, (0, n - g.shape[0])).reshape(1, n)
    lane = jnp.arange(HEAD_PAD)
    half = MLA_ROPE // 2
    inv_freq = jnp.where((lane >= MLA_NOPE) & (lane < MLA_QK),
                         ROPE_THETA ** (-((lane - MLA_NOPE) % half).astype(F32) / half), 0.0).reshape(1, HEAD_PAD)
    head_sum = jnp.repeat(jnp.eye(HEADS, dtype=F32), 64, axis=0)
    return dict(
        w_in=w_in_p,
        mla_q_gain=mla_q_gain[l].reshape(1, -1),
        mla_w_uq=pad_heads(mla_w_uq[l], MLA_QK).astype(BF16),
        mla_kv_gain=mla_kv_gain[l].reshape(1, -1),
        mla_w_uk=pad_heads(ukv[:, :, :MLA_NOPE].reshape(MLA_KV_RANK, -1), MLA_NOPE).astype(BF16),
        mla_w_uv=ukv[:, :, MLA_NOPE:].reshape(MLA_KV_RANK, -1).astype(BF16),
        mla_qn_gain=pad1(mla_qn_gain[l], HEAD_PAD),
        mla_kn_gain=pad1(mla_kn_gain[l], HEAD_PAD),
        rope_inv_freq=inv_freq,
        conv_wq=gdn_conv_w[l][:, 0:512], conv_wk=gdn_conv_w[l][:, 512:1024], conv_wv=gdn_conv_w[l][:, 1024:1536],
        gdn_a_log=gdn_a_log[l].reshape(1, -1), gdn_dt_bias=gdn_dt_bias[l].reshape(1, -1),
        gdn_o_gain=gdn_o_gain[l].reshape(1, -1),
        moba_qn_gain=jnp.tile(moba_qn_gain[l], HEADS).reshape(1, -1),
        moba_kn_gain=jnp.tile(moba_kn_gain[l], HEADS).reshape(1, -1),
        head_sum=head_sum, head_expand=head_sum.T,
        w_branch=w_branch[l].astype(BF16), w_out=w_out[l].astype(BF16),
        peer_w_query=peer_w_query[l].astype(BF16),
        peer_sub_keys=peer_sub_keys[l].reshape(2 * PEER_HEADS, PEER_NKEYS, -1),
    )


def kernel(x, c, positions, w_mod, b_mod, w_in, mla_q_gain, mla_w_uq, mla_kv_gain, mla_w_ukv, mla_qn_gain, mla_kn_gain, gdn_conv_w, gdn_a_log, gdn_dt_bias, gdn_o_gain, moba_qn_gain, moba_kn_gain, w_branch, w_out, peer_w_query, peer_sub_keys, peer_u, peer_v):
    bsz, seq, d = x.shape
    t = bsz * seq
    depth = w_mod.shape[0]
    xt = x.reshape(t, d)
    pos = positions.reshape(t, 1)
    mod = _mod_all(c, w_mod, b_mod)
    for l in range(depth):
        lw = _prep_layer_weights(l, w_in, mla_q_gain, mla_w_uq, mla_kv_gain, mla_w_ukv, mla_qn_gain, mla_kn_gain,
                                 gdn_conv_w, gdn_a_log, gdn_dt_bias, gdn_o_gain, moba_qn_gain, moba_kn_gain,
                                 w_branch, w_out, peer_w_query, peer_sub_keys)
        sh_a, sc_a, g_a, sh_f, sc_f, g_f = [mod[l, :, i * d:(i + 1) * d] for i in range(6)]
        proj, _ = _modulate_matmul(xt, sh_a, sc_a, lw["w_in"], seq)
        q, k, v = _mla_prep(proj, pos, lw)
        o_mla = _causal_attention(q, k, v, bsz, seq)
        q, k, v = _moba_prep(proj, lw, seq)
        o_moba = _causal_attention(q, k, v, bsz, seq)
        qn, kn, vv, gb = _gdn_prep(proj, lw, seq)
        o_gdn = _gdn_scan(qn, kn, vv, gb, proj, lw, bsz, seq)
        xt = _merge(o_mla, o_gdn, o_moba, proj, xt, g_a, lw, seq)
        qry, h2 = _modulate_matmul(xt, sh_f, sc_f, lw["peer_w_query"], seq)
        ids, gates = _peer_select(qry, lw["peer_sub_keys"])
        xt = _peer_apply(ids, gates, h2, xt, g_f, peer_u[l], peer_v[l], seq)
    return xt.reshape(bsz, seq, d)
```

```python
import functools

import jax
import jax.numpy as jnp
from jax import lax
from jax.experimental import pallas as pl
from jax.experimental.pallas import tpu as pltpu

F32 = jnp.float32
BF16 = jnp.bfloat16
HI = lax.Precision.HIGHEST

D_MODEL = 1024
MLA_HEADS = 8
MLA_Q_RANK = 256
MLA_KV_RANK = 128
MLA_NOPE = 64
MLA_ROPE = 32
MLA_V = 64
MLA_QK = MLA_NOPE + MLA_ROPE
ROPE_THETA = 10000.0
HEADS = 8
HEAD_PAD = 128
GDN_DK = 64
GDN_DV = 64
GDN_CONV = 4
GDN_CHUNK = 64
MOBA_DH = 64
MOBA_BLOCK = 256
MOBA_TOPK = 3
N_BRANCH = 3
BRANCH_W = 512
PEER_HEADS = 8
PEER_NKEYS = 128
PEER_TOPK = 16
PEER_SEL = PEER_HEADS * PEER_TOPK
NORM_EPS = 1e-6
NEG_INF = -1e30

COL_LAT = 0
COL_GQ = 512
COL_MQ = 2560
COL_GATE = 4096
IN_COLS_PAD = 7168

VMEM_LIMIT = 56 * 1024 * 1024


def _cp(sem, vmem=None):
    return pltpu.CompilerParams(dimension_semantics=sem, vmem_limit_bytes=vmem or VMEM_LIMIT)


def _nt(a, b, precision=None):
    return lax.dot_general(a, b, (((1,), (1,)), ((), ())), precision=precision, preferred_element_type=F32)


def _tn(a, b, precision=None):
    return lax.dot_general(a, b, (((0,), (0,)), ((), ())), precision=precision, preferred_element_type=F32)


def _mm(a, b, precision=None):
    return jnp.dot(a, b, precision=precision, preferred_element_type=F32)


def _silu(x):
    return x * jax.nn.sigmoid(x)


def _mod_kernel(c_ref, w_ref, b_ref, o_ref):
    c = c_ref[...]
    o_ref[0] = _mm(_silu(c), w_ref[0], HI) + b_ref[0]


def _mod_all(c, w_mod, b_mod):
    depth, d, n = w_mod.shape
    bsz = c.shape[0]
    tn = 1024
    return pl.pallas_call(
        _mod_kernel,
        grid=(depth, n // tn),
        in_specs=[pl.BlockSpec((bsz, d), lambda l, j: (0, 0)),
                  pl.BlockSpec((1, d, tn), lambda l, j: (l, 0, j)),
                  pl.BlockSpec((1, 1, tn), lambda l, j: (l, 0, j))],
        out_specs=pl.BlockSpec((1, bsz, tn), lambda l, j: (l, 0, j)),
        out_shape=jax.ShapeDtypeStruct((depth, bsz, n), F32),
        compiler_params=_cp(("arbitrary", "arbitrary")),
        name="adaln_mod",
    )(c, w_mod, b_mod.reshape(depth, 1, n))


def _modmm_kernel(x_ref, sh_ref, sc_ref, w_ref, o_ref, h_ref, hb_scr):
    @pl.when(pl.program_id(1) == 0)
    def _():
        x = x_ref[...]
        h = x * lax.rsqrt(jnp.mean(x * x, axis=-1, keepdims=True) + NORM_EPS)
        h = h * (1.0 + sc_ref[0]) + sh_ref[0]
        h_ref[...] = h
        hb_scr[...] = h.astype(BF16)

    o_ref[...] = _mm(hb_scr[...], w_ref[...])


def _modulate_matmul(x, shift, scale, w_bf16, seq, tm=512, tn=1024):
    t, d = x.shape
    n = w_bf16.shape[1]
    bsz = shift.shape[0]
    bidx = lambda i, j: ((i * tm) // seq, 0, 0)
    return pl.pallas_call(
        _modmm_kernel,
        grid=(t // tm, n // tn),
        in_specs=[pl.BlockSpec((tm, d), lambda i, j: (i, 0)),
                  pl.BlockSpec((1, 1, d), bidx),
                  pl.BlockSpec((1, 1, d), bidx),
                  pl.BlockSpec((d, tn), lambda i, j: (0, j))],
        out_specs=[pl.BlockSpec((tm, tn), lambda i, j: (i, j)),
                   pl.BlockSpec((tm, d), lambda i, j: (i, 0))],
        out_shape=[jax.ShapeDtypeStruct((t, n), F32), jax.ShapeDtypeStruct((t, d), F32)],
        scratch_shapes=[pltpu.VMEM((tm, d), BF16)],
        compiler_params=_cp(("arbitrary", "arbitrary")),
        name="modulate_matmul",
    )(x, shift.reshape(bsz, 1, d), scale.reshape(bsz, 1, d), w_bf16)


def _mla_prep_kernel(p_ref, pos_ref, qg_ref, wuq_ref, kvg_ref, wuk_ref, wuv_ref, qng_ref, kng_ref, invf_ref,
                     q_out, k_out, v_out):
    tm = p_ref.shape[0]
    cq = p_ref[:, 0:MLA_Q_RANK]
    ckv = p_ref[:, MLA_Q_RANK:MLA_Q_RANK + MLA_KV_RANK]
    misc = p_ref[:, MLA_Q_RANK + MLA_KV_RANK:MLA_Q_RANK + MLA_KV_RANK + 128]

    def rms(v, n):
        return v * lax.rsqrt(jnp.sum(v * v, axis=-1, keepdims=True) * (1.0 / n) + NORM_EPS)

    qn = (rms(cq, MLA_Q_RANK) * qg_ref[...]).astype(BF16)
    q_all = _mm(qn, wuq_ref[...])
    kvn = (rms(ckv, MLA_KV_RANK) * kvg_ref[...]).astype(BF16)
    k_all = _mm(kvn, wuk_ref[...])
    v_out[...] = _mm(kvn, wuv_ref[...]).astype(BF16)

    lane = lax.broadcasted_iota(jnp.int32, (tm, HEAD_PAD), 1)
    in_rope = (lane >= MLA_NOPE) & (lane < MLA_QK)
    k_rope = jnp.where(in_rope, pltpu.roll(misc, MLA_NOPE, 1), 0.0)
    ang = pos_ref[...].astype(F32) * invf_ref[...]
    cos = jnp.cos(ang)
    sin = jnp.sin(ang)
    half = MLA_ROPE // 2
    c_tab = jnp.where(lane < MLA_NOPE, 1.0, jnp.where(in_rope, cos, 0.0))
    s_lo = jnp.where(in_rope & (lane < MLA_NOPE + half), -sin, 0.0)
    s_hi = jnp.where(in_rope & (lane >= MLA_NOPE + half), sin, 0.0)

    def finish(xh, gain):
        xh = xh * lax.rsqrt(jnp.sum(xh * xh, axis=-1, keepdims=True) * (1.0 / MLA_QK) + NORM_EPS) * gain
        return xh * c_tab + pltpu.roll(xh, HEAD_PAD - half, 1) * s_lo + pltpu.roll(xh, half, 1) * s_hi

    scale = MLA_QK ** -0.5
    for h in range(MLA_HEADS):
        sl = slice(h * HEAD_PAD, (h + 1) * HEAD_PAD)
        q_out[:, sl] = (finish(q_all[:, sl], qng_ref[...]) * scale).astype(BF16)
        k_out[:, sl] = finish(k_all[:, sl] + k_rope, kng_ref[...]).astype(BF16)


def _mla_prep(proj, pos, lw, tm=256):
    t = proj.shape[0]
    full = lambda shp: pl.BlockSpec(shp, lambda i: (0,) * len(shp))
    return pl.pallas_call(
        _mla_prep_kernel,
        grid=(t // tm,),
        in_specs=[pl.BlockSpec((tm, 512), lambda i: (i, 0)),
                  pl.BlockSpec((tm, 1), lambda i: (i, 0)),
                  full((1, MLA_Q_RANK)), full((MLA_Q_RANK, HEADS * HEAD_PAD)),
                  full((1, MLA_KV_RANK)), full((MLA_KV_RANK, HEADS * HEAD_PAD)),
                  full((MLA_KV_RANK, HEADS * MLA_V)),
                  full((1, HEAD_PAD)), full((1, HEAD_PAD)), full((1, HEAD_PAD))],
        out_specs=[pl.BlockSpec((tm, HEADS * HEAD_PAD), lambda i: (i, 0)),
                   pl.BlockSpec((tm, HEADS * HEAD_PAD), lambda i: (i, 0)),
                   pl.BlockSpec((tm, HEADS * MLA_V), lambda i: (i, 0))],
        out_shape=[jax.ShapeDtypeStruct((t, HEADS * HEAD_PAD), BF16),
                   jax.ShapeDtypeStruct((t, HEADS * HEAD_PAD), BF16),
                   jax.ShapeDtypeStruct((t, HEADS * MLA_V), BF16)],
        compiler_params=_cp(("arbitrary",)),
        name="mla_prep",
    )(proj, pos, lw["mla_q_gain"], lw["mla_w_uq"], lw["mla_kv_gain"], lw["mla_w_uk"], lw["mla_w_uv"],
      lw["mla_qn_gain"], lw["mla_kn_gain"], lw["rope_inv_freq"])


def _attn_kernel(q_ref, k_ref, v_ref, o_ref, *, tq):
    i = pl.program_id(2)
    row = lax.broadcasted_iota(jnp.int32, (tq, tq), 0)
    col = lax.broadcasted_iota(jnp.int32, (tq, tq), 1)
    outs = []
    for hh in range(2):
        q = q_ref[:, hh * HEAD_PAD:(hh + 1) * HEAD_PAD]

        def step(j, carry, masked, hh=hh, q=q):
            m, l, acc = carry
            start = pl.multiple_of(j * tq, tq)
            kk = k_ref[pl.ds(start, tq), hh * HEAD_PAD:(hh + 1) * HEAD_PAD]
            vv = v_ref[pl.ds(start, tq), :]
            s = _nt(q, kk)
            if masked:
                s = jnp.where(col <= row, s, NEG_INF)
            m_new = jnp.maximum(m, jnp.max(s, axis=-1, keepdims=True))
            p = jnp.exp(s - m_new)
            alpha = jnp.exp(m - m_new)
            l = alpha * l + jnp.sum(p, axis=-1, keepdims=True)
            acc = alpha * acc + _mm(p.astype(BF16), vv)
            return m_new, l, acc

        init = (jnp.full((tq, 1), -jnp.inf, F32), jnp.zeros((tq, 1), F32), jnp.zeros((tq, 2 * MLA_V), F32))
        carry = lax.fori_loop(0, i, functools.partial(step, masked=False), init)
        m, l, acc = step(i, carry, True)
        outs.append(acc / l)
    lane = lax.broadcasted_iota(jnp.int32, (tq, 2 * MLA_V), 1)
    o_ref[...] = jnp.where(lane < MLA_V, outs[0], outs[1])


def _causal_attention(q, k, v, bsz, seq, tq=512):
    t = q.shape[0]
    nq = seq // tq
    return pl.pallas_call(
        functools.partial(_attn_kernel, tq=tq),
        grid=(bsz, HEADS // 2, nq),
        in_specs=[pl.BlockSpec((tq, 2 * HEAD_PAD), lambda b, hp, i: (b * nq + i, hp)),
                  pl.BlockSpec((seq, 2 * HEAD_PAD), lambda b, hp, i: (b, hp)),
                  pl.BlockSpec((seq, 2 * MLA_V), lambda b, hp, i: (b, hp))],
        out_specs=pl.BlockSpec((tq, 2 * MLA_V), lambda b, hp, i: (b * nq + i, hp)),
        out_shape=jax.ShapeDtypeStruct((t, HEADS * MLA_V), F32),
        compiler_params=_cp(("arbitrary", "arbitrary", "arbitrary")),
        name="causal_attention",
    )(q, k, v)


def _moba_prep_kernel(mq_ref, mk_ref, mv_ref, qg_ref, kg_ref, e_ref, et_ref, q_out, k_out, v_out, kmean_scr, *, nb):
    tm = mq_ref.shape[0]
    n = pl.program_id(0) % nb

    @pl.when(n == 0)
    def _():
        kmean_scr[...] = jnp.zeros_like(kmean_scr)

    def headnorm(x, gain):
        ss = _mm(x * x, e_ref[...], HI)
        inv = lax.rsqrt(ss * (1.0 / MOBA_DH) + NORM_EPS)
        return x * _mm(inv, et_ref[...], HI) * gain

    qn = headnorm(mq_ref[...], qg_ref[...])
    kn = headnorm(mk_ref[...], kg_ref[...])
    v_out[...] = mv_ref[...].astype(BF16)
    kmean_scr[pl.ds(n, 1), :] = jnp.mean(kn, axis=0, keepdims=True)
    km = kmean_scr[...]

    lane = lax.broadcasted_iota(jnp.int32, (tm, nb), 1)
    zpad = jnp.zeros((tm, HEAD_PAD - MOBA_DH - nb), F32)
    onehot = jnp.where(lane == n, 1.0, 0.0)
    for h in range(HEADS):
        sl = slice(h * MOBA_DH, (h + 1) * MOBA_DH)
        gate = _nt(qn[:, sl], km[:, sl], HI)
        gate = jnp.where(lane < n, gate, -jnp.inf)
        pen = jnp.full((tm, nb), NEG_INF, F32)
        for r in range(MOBA_TOPK):
            mx = jnp.max(gate, axis=-1, keepdims=True)
            idx = jnp.min(jnp.where(gate == mx, lane, nb), axis=-1, keepdims=True)
            hit = (lane == idx) & (r < n)
            pen = jnp.where(hit, 0.0, pen)
            gate = jnp.where(lane == idx, -jnp.inf, gate)
        pen = jnp.where(lane == n, 0.0, pen)
        osl = slice(h * HEAD_PAD, (h + 1) * HEAD_PAD)
        q_out[:, osl] = jnp.concatenate([qn[:, sl] * (MOBA_DH ** -0.5), pen, zpad], axis=1).astype(BF16)
        k_out[:, osl] = jnp.concatenate([kn[:, sl], onehot, zpad], axis=1).astype(BF16)


def _moba_prep(proj, lw, seq):
    t = proj.shape[0]
    tm = MOBA_BLOCK
    nb = seq // tm
    full = lambda shp: pl.BlockSpec(shp, lambda i: (0,) * len(shp))
    c0 = COL_MQ // 512
    return pl.pallas_call(
        functools.partial(_moba_prep_kernel, nb=nb),
        grid=(t // tm,),
        in_specs=[pl.BlockSpec((tm, 512), lambda i: (i, c0)),
                  pl.BlockSpec((tm, 512), lambda i: (i, c0 + 1)),
                  pl.BlockSpec((tm, 512), lambda i: (i, c0 + 2)),
                  full((1, 512)), full((1, 512)), full((512, HEADS)), full((HEADS, 512))],
        out_specs=[pl.BlockSpec((tm, HEADS * HEAD_PAD), lambda i: (i, 0)),
                   pl.BlockSpec((tm, HEADS * HEAD_PAD), lambda i: (i, 0)),
                   pl.BlockSpec((tm, HEADS * MOBA_DH), lambda i: (i, 0))],
        out_shape=[jax.ShapeDtypeStruct((t, HEADS * HEAD_PAD), BF16),
                   jax.ShapeDtypeStruct((t, HEADS * HEAD_PAD), BF16),
                   jax.ShapeDtypeStruct((t, HEADS * MOBA_DH), BF16)],
        scratch_shapes=[pltpu.VMEM((nb, HEADS * MOBA_DH), F32)],
        compiler_params=_cp(("arbitrary",)),
        name="moba_prep",
    )(proj, proj, proj, lw["moba_qn_gain"], lw["moba_kn_gain"], lw["head_sum"], lw["head_expand"])


def _gdn_prep_kernel(gq_ref, gk_ref, gv_ref, misc_ref, wq_ref, wk_ref, wv_ref, alog_ref, dtb_ref, e_ref, et_ref,
                     q_out, k_out, v_out, gb_out, ext_scr, *, tiles_per_seq):
    tm = gq_ref.shape[0]
    pad = 8

    @pl.when(pl.program_id(0) % tiles_per_seq == 0)
    def _():
        ext_scr[:, 0:pad, :] = jnp.zeros((3, pad, ext_scr.shape[2]), F32)

    def conv_silu(s, x_ref, w_ref):
        ext_scr[s, pad:pad + tm, :] = x_ref[...]
        y = jnp.zeros(x_ref.shape, F32)
        for i in range(GDN_CONV):
            y = y + ext_scr[s, pl.ds(pad - (GDN_CONV - 1) + i, tm), :] * w_ref[i:i + 1, :]
        ext_scr[s, 0:pad, :] = ext_scr[s, tm:tm + pad, :]
        return _silu(y)

    def l2n(x):
        ss = _mm(x * x, e_ref[...], HI)
        return x * _mm(lax.rsqrt(ss + NORM_EPS), et_ref[...], HI)

    q_out[...] = l2n(conv_silu(0, gq_ref, wq_ref)) * (GDN_DK ** -0.5)
    k_out[...] = l2n(conv_silu(1, gk_ref, wk_ref))
    v_out[...] = conv_silu(2, gv_ref, wv_ref)
    misc = misc_ref[...]
    a = misc[:, MLA_ROPE:MLA_ROPE + HEADS]
    b = misc[:, MLA_ROPE + HEADS:MLA_ROPE + 2 * HEADS]
    g = -jnp.exp(alog_ref[...]) * jax.nn.softplus(a + dtb_ref[...])
    beta = jax.nn.sigmoid(b)
    gb_out[...] = jnp.concatenate([g, beta, jnp.zeros((tm, 128 - 2 * HEADS), F32)], axis=1)


def _gdn_prep(proj, lw, seq, tm=256):
    t = proj.shape[0]
    full = lambda shp: pl.BlockSpec(shp, lambda i: (0,) * len(shp))
    c0 = COL_GQ // 512
    tok = lambda w: pl.BlockSpec((tm, w), lambda i: (i, 0))
    return pl.pallas_call(
        functools.partial(_gdn_prep_kernel, tiles_per_seq=seq // tm),
        grid=(t // tm,),
        in_specs=[pl.BlockSpec((tm, 512), lambda i: (i, c0)),
                  pl.BlockSpec((tm, 512), lambda i: (i, c0 + 1)),
                  pl.BlockSpec((tm, 512), lambda i: (i, c0 + 2)),
                  pl.BlockSpec((tm, 128), lambda i: (i, 3)),
                  full((GDN_CONV, 512)), full((GDN_CONV, 512)), full((GDN_CONV, 512)),
                  full((1, HEADS)), full((1, HEADS)), full((512, HEADS)), full((HEADS, 512))],
        out_specs=[tok(512), tok(512), tok(512), tok(128)],
        out_shape=[jax.ShapeDtypeStruct((t, 512), F32)] * 3 + [jax.ShapeDtypeStruct((t, 128), F32)],
        scratch_shapes=[pltpu.VMEM((3, tm + 8, 512), F32)],
        compiler_params=_cp(("arbitrary",)),
        name="gdn_prep",
    )(proj, proj, proj, proj, lw["conv_wq"], lw["conv_wk"], lw["conv_wv"], lw["gdn_a_log"], lw["gdn_dt_bias"],
      lw["head_sum"], lw["head_expand"])


def _gdn_scan_kernel(q_ref, k_ref, v_ref, gb_ref, z_ref, og_ref, o_ref, s_scr, *, chunks):
    c = GDN_CHUNK

    @pl.when(pl.program_id(1) == 0)
    def _():
        s_scr[...] = jnp.zeros_like(s_scr)

    ri = lax.broadcasted_iota(jnp.int32, (c, c), 0)
    ci = lax.broadcasted_iota(jnp.int32, (c, c), 1)
    tri = ci <= ri
    strict = ci < ri
    ltri = jnp.where(tri, 1.0, 0.0)
    probs = [(cc, h) for cc in range(chunks) for h in range(HEADS)]
    rows = lambda cc: slice(cc * c, (cc + 1) * c)
    hsl = lambda h: slice(h * GDN_DK, (h + 1) * GDN_DK)
    gbs = [gb_ref[rows(cc), :] for cc in range(chunks)]
    gcs = [_mm(ltri, gb[:, 0:HEADS], HI) for gb in gbs]
    gcts = [g.T for g in gcs]
    q = [q_ref[rows(cc), hsl(h)] for cc, h in probs]
    k = [k_ref[rows(cc), hsl(h)] for cc, h in probs]
    v = [v_ref[rows(cc), hsl(h)] for cc, h in probs]
    gcol = [gcs[cc][:, h:h + 1] for cc, h in probs]
    grow = [gcts[cc][h:h + 1, :] for cc, h in probs]
    glast = [gcs[cc][c - 1:c, h:h + 1] for cc, h in probs]
    bcol = [gbs[cc][:, HEADS + h:HEADS + h + 1] for cc, h in probs]
    n = len(probs)
    egc = [jnp.exp(g) for g in gcol]
    decay = [jnp.where(tri, jnp.exp(jnp.where(tri, gcol[i] - grow[i], 0.0)), 0.0) for i in range(n)]
    kb = [k[i] * bcol[i] for i in range(n)]
    k16 = [a.astype(BF16) for a in k]
    kb16 = [a.astype(BF16) for a in kb]
    a_neg = [jnp.where(strict, -(_nt(kb16[i], k16[i]) * decay[i]), 0.0) for i in range(n)]
    x = [jnp.concatenate([v[i] * bcol[i], kb[i] * egc[i]], axis=1) for i in range(n)]
    for lvl in range(6):
        a16 = [a.astype(BF16) for a in a_neg]
        if lvl < 5:
            r = [_mm(a16[i], jnp.concatenate([a_neg[i], x[i]], axis=1).astype(BF16)) for i in range(n)]
            a_neg = [ri_[:, 0:c] for ri_ in r]
            x = [x[i] + r[i][:, c:] for i in range(n)]
        else:
            x = [x[i] + _mm(a16[i], x[i].astype(BF16)) for i in range(n)]
    u = [xi[:, 0:GDN_DV] for xi in x]
    w16 = [xi[:, GDN_DV:].astype(BF16) for xi in x]
    attn16 = [(_nt(q[i].astype(BF16), k16[i]) * decay[i]).astype(BF16) for i in range(n)]
    qe16 = [(q[i] * egc[i]).astype(BF16) for i in range(n)]
    kd16 = [(k[i] * jnp.exp(glast[i] - gcol[i])).astype(BF16) for i in range(n)]
    eglast = [jnp.exp(g) for g in glast]
    state = [s_scr[h] for h in range(HEADS)]
    for cc in range(chunks):
        ids = [cc * HEADS + h for h in range(HEADS)]
        s16 = [s.astype(BF16) for s in state]
        v_new = [u[i] - _mm(w16[i], s16[h]) for h, i in enumerate(ids)]
        vn16 = [a.astype(BF16) for a in v_new]
        o = [_mm(qe16[i], s16[h]) + _mm(attn16[i], vn16[h]) for h, i in enumerate(ids)]
        state = [state[h] * eglast[i] + _tn(kd16[i], vn16[h]) for h, i in enumerate(ids)]
        o = [oh * lax.rsqrt(jnp.mean(oh * oh, axis=-1, keepdims=True) + NORM_EPS) * og_ref[...] for oh in o]
        o_ref[rows(cc), :] = jnp.concatenate(o, axis=1) * _silu(z_ref[rows(cc), :])
    for h in range(HEADS):
        s_scr[h] = state[h]


def _gdn_scan(qn, kn, vv, gb, proj, lw, bsz, seq, chunks=4):
    t = qn.shape[0]
    ct = chunks * GDN_CHUNK
    ns = seq // ct
    tok = lambda w: pl.BlockSpec((ct, w), lambda b, i: (b * ns + i, 0))
    return pl.pallas_call(
        functools.partial(_gdn_scan_kernel, chunks=chunks),
        grid=(bsz, ns),
        in_specs=[tok(512), tok(512), tok(512), tok(128),
                  pl.BlockSpec((ct, 512), lambda b, i: (b * ns + i, COL_GQ // 512 + 3)),
                  pl.BlockSpec((1, GDN_DV), lambda b, i: (0, 0))],
        out_specs=tok(512),
        out_shape=jax.ShapeDtypeStruct((t, 512), F32),
        scratch_shapes=[pltpu.VMEM((HEADS, GDN_DK, GDN_DV), F32)],
        compiler_params=_cp(("arbitrary", "arbitrary")),
        name="gdn_scan",
    )(qn, kn, vv, gb, proj, lw["gdn_o_gain"])


def _merge_kernel(oa_ref, ob_ref, oc_ref, g0_ref, g1_ref, g2_ref, x_ref, ga_ref, wb_ref, wo_ref, o_ref):
    y = None
    for o_n, g_n, n in ((oa_ref, g0_ref, 0), (ob_ref, g1_ref, 1), (oc_ref, g2_ref, 2)):
        term = jax.nn.sigmoid(g_n[...]) * _mm(o_n[...].astype(BF16), wb_ref[n])
        y = term if y is None else y + term
    o_ref[...] = x_ref[...] + ga_ref[0] * _mm(y.astype(BF16), wo_ref[...])


def _merge(o_mla, o_gdn, o_moba, proj, x, g_a, lw, seq, tm=512):
    t, d = x.shape
    bsz = g_a.shape[0]
    tok = lambda w: pl.BlockSpec((tm, w), lambda i: (i, 0))
    gcol = COL_GATE // d
    return pl.pallas_call(
        _merge_kernel,
        grid=(t // tm,),
        in_specs=[tok(BRANCH_W), tok(BRANCH_W), tok(BRANCH_W),
                  pl.BlockSpec((tm, d), lambda i: (i, gcol)),
                  pl.BlockSpec((tm, d), lambda i: (i, gcol + 1)),
                  pl.BlockSpec((tm, d), lambda i: (i, gcol + 2)),
                  tok(d),
                  pl.BlockSpec((1, 1, d), lambda i: ((i * tm) // seq, 0, 0)),
                  pl.BlockSpec((N_BRANCH, BRANCH_W, d), lambda i: (0, 0, 0)),
                  pl.BlockSpec((d, d), lambda i: (0, 0))],
        out_specs=tok(d),
        out_shape=jax.ShapeDtypeStruct((t, d), F32),
        compiler_params=_cp(("arbitrary",)),
        name="branch_merge",
    )(o_mla, o_gdn, o_moba, proj, proj, proj, x, g_a.reshape(bsz, 1, d), lw["w_branch"], lw["w_out"])


def _topk_rows(s, k):
    n = s.shape[0]
    ri = lax.broadcasted_iota(jnp.int32, s.shape, 0)
    vals, idxs = [], []
    for _ in range(k):
        mx = jnp.max(s, axis=0, keepdims=True)
        idx = jnp.min(jnp.where(s == mx, ri, n), axis=0, keepdims=True)
        vals.append(mx)
        idxs.append(idx)
        s = jnp.where(ri == idx, -jnp.inf, s)
    return jnp.concatenate(vals, axis=0), jnp.concatenate(idxs, axis=0)


def _peer_select_kernel(qry_ref, keys_ref, ids_out, gates_out):
    kk = PEER_TOPK
    ids_rows, gate_rows = [], []
    for h in range(PEER_HEADS):
        halves = []
        for p in range(2):
            g = h * 2 + p
            s_t = _nt(keys_ref[g], qry_ref[:, g * 128:(g + 1) * 128], HI)
            halves.append(_topk_rows(s_t, kk))
        (v1, i1), (v2, i2) = halves
        cand, a_of, b_of = [], [], []
        for a in range(kk):
            nb = kk // (a + 1)
            cand.append(v1[a:a + 1, :] + v2[0:nb, :])
            a_of += [a] * nb
            b_of += list(range(nb))
        cand = jnp.concatenate(cand, axis=0)
        top_s, top_r = _topk_rows(cand, kk)
        a_sel = jnp.zeros_like(top_r)
        b_sel = jnp.zeros_like(top_r)
        for r, (a, b) in enumerate(zip(a_of, b_of)):
            hit = top_r == r
            a_sel = jnp.where(hit, a, a_sel)
            b_sel = jnp.where(hit, b, b_sel)
        e1 = jnp.zeros_like(top_r)
        e2 = jnp.zeros_like(top_r)
        for a in range(kk):
            e1 = jnp.where(a_sel == a, i1[a:a + 1, :], e1)
            e2 = jnp.where(b_sel == a, i2[a:a + 1, :], e2)
        ids_rows.append(e1 * PEER_NKEYS + e2)
        ex = jnp.exp(top_s - top_s[0:1, :])
        gate_rows.append(ex / jnp.sum(ex, axis=0, keepdims=True))
    ids_out[...] = jnp.concatenate(ids_rows, axis=0).T
    gates_out[...] = jnp.concatenate(gate_rows, axis=0).T


def _peer_select(qry, sub_keys, tm=256):
    t = qry.shape[0]
    return pl.pallas_call(
        _peer_select_kernel,
        grid=(t // tm,),
        in_specs=[pl.BlockSpec((tm, 2 * PEER_HEADS * 128), lambda i: (i, 0)),
                  pl.BlockSpec((2 * PEER_HEADS, PEER_NKEYS, 128), lambda i: (0, 0, 0))],
        out_specs=[pl.BlockSpec((tm, PEER_SEL), lambda i: (i, 0)),
                   pl.BlockSpec((tm, PEER_SEL), lambda i: (i, 0))],
        out_shape=[jax.ShapeDtypeStruct((t, PEER_SEL), jnp.int32), jax.ShapeDtypeStruct((t, PEER_SEL), F32)],
        compiler_params=_cp(("arbitrary",)),
        name="peer_select",
    )(qry, sub_keys)


def _peer_apply_kernel(ids_ref, ids_next_ref, gates_ref, h_ref, x_ref, gf_ref, uv_hbm, o_ref, buf, sems, *, tt):
    s = pl.program_id(0)
    ns = pl.num_programs(0)
    rows = tt * PEER_SEL

    def start_gather(idr, slot):
        def tok(t, carry):
            for e in range(PEER_SEL):
                pltpu.make_async_copy(uv_hbm.at[pl.ds(idr[t, e], 1)], buf.at[slot, t, pl.ds(e, 1)],
                                      sems.at[slot]).start(priority=e % 2)
            return carry
        lax.fori_loop(0, tt, tok, 0)

    def wait_gather(slot):
        pltpu.make_async_copy(buf.at[1 - slot], buf.at[slot], sems.at[slot]).wait()

    slot = s % 2

    @pl.when(s == 0)
    def _():
        start_gather(ids_ref, 0)

    @pl.when(s + 1 < ns)
    def _():
        start_gather(ids_next_ref, 1 - slot)

    wait_gather(slot)
    d = h_ref.shape[1]
    hb = h_ref[...]
    ub = buf[slot, :, :, 0:d]
    act = jnp.sum(ub * hb[:, None, :], axis=-1)
    gel = 0.5 * act * (1.0 + lax.erf(act * (2.0 ** -0.5)))
    wgt = gates_ref[...] * gel
    vb = buf[slot, :, :, d:2 * d]
    out = jnp.sum(wgt[:, :, None] * vb, axis=1)
    o_ref[...] = x_ref[...] + gf_ref[0] * out


def _peer_apply(ids, gates, h2, x, g_f, uv, seq, tt=16):
    t, d = x.shape
    bsz = g_f.shape[0]
    ns = t // tt
    tok = lambda w: pl.BlockSpec((tt, w), lambda i: (i, 0))
    return pl.pallas_call(
        functools.partial(_peer_apply_kernel, tt=tt),
        grid=(ns,),
        in_specs=[pl.BlockSpec((tt, PEER_SEL), lambda i: (i, 0), memory_space=pltpu.SMEM),
                  pl.BlockSpec((tt, PEER_SEL), lambda i: (jnp.minimum(i + 1, ns - 1), 0), memory_space=pltpu.SMEM),
                  tok(PEER_SEL), tok(d), tok(d),
                  pl.BlockSpec((1, 1, d), lambda i: ((i * tt) // seq, 0, 0)),
                  pl.BlockSpec(memory_space=pl.ANY)],
        out_specs=tok(d),
        out_shape=jax.ShapeDtypeStruct((t, d), F32),
        scratch_shapes=[pltpu.VMEM((2, tt, PEER_SEL, 2 * d), F32),
                        pltpu.SemaphoreType.DMA((2,))],
        compiler_params=_cp(("arbitrary",)),
        name="peer_apply",
    )(ids, ids, gates, h2, x, g_f.reshape(bsz, 1, d), uv)


def _prep_layer_weights(l, w_in, mla_q_gain, mla_w_uq, mla_kv_gain, mla_w_ukv, mla_qn_gain, mla_kn_gain,
                        gdn_conv_w, gdn_a_log, gdn_dt_bias, gdn_o_gain, moba_qn_gain, moba_kn_gain,
                        w_branch, w_out, peer_w_query, peer_sub_keys):
    d = D_MODEL
    w = w_in[l]
    o = 0
    parts = {}
    for name, width in (("cq", 256), ("ckv", 128), ("kr", 32), ("gq", 512), ("gk", 512), ("gv", 512), ("gz", 512),
                        ("ga", 8), ("gb", 8), ("mq", 512), ("mk", 512), ("mv", 512), ("gate", 3072)):
        parts[name] = w[:, o:o + width]
        o += width
    w_in_p = jnp.concatenate(
        [parts["cq"], parts["ckv"], parts["kr"], parts["ga"], parts["gb"], jnp.zeros((d, 80), F32),
         parts["gq"], parts["gk"], parts["gv"], parts["gz"], parts["mq"], parts["mk"], parts["mv"], parts["gate"]],
        axis=1).astype(BF16)

    def pad_heads(m, width):
        r = m.shape[0]
        return jnp.pad(m.reshape(r, HEADS, width), ((0, 0), (0, 0), (0, HEAD_PAD - width))).reshape(r, HEADS * HEAD_PAD)

    ukv = mla_w_ukv[l].reshape(MLA_KV_RANK, HEADS, MLA_NOPE + MLA_V)
    pad1 = lambda g, n: jnp.pad(g, (0, n - g.shape[0])).reshape(1, n)
    lane = jnp.arange(HEAD_PAD)
    half = MLA_ROPE // 2
    inv_freq = jnp.where((lane >= MLA_NOPE) & (lane < MLA_QK),
                         ROPE_THETA ** (-((lane - MLA_NOPE) % half).astype(F32) / half), 0.0).reshape(1, HEAD_PAD)
    head_sum = jnp.repeat(jnp.eye(HEADS, dtype=F32), 64, axis=0)
    return dict(
        w_in=w_in_p,
        mla_q_gain=mla_q_gain[l].reshape(1, -1),
        mla_w_uq=pad_heads(mla_w_uq[l], MLA_QK).astype(BF16),
        mla_kv_gain=mla_kv_gain[l].reshape(1, -1),
        mla_w_uk=pad_heads(ukv[:, :, :MLA_NOPE].reshape(MLA_KV_RANK, -1), MLA_NOPE).astype(BF16),
        mla_w_uv=ukv[:, :, MLA_NOPE:].reshape(MLA_KV_RANK, -1).astype(BF16),
        mla_qn_gain=pad1(mla_qn_gain[l], HEAD_PAD),
        mla_kn_gain=pad1(mla_kn_gain[l], HEAD_PAD),
        rope_inv_freq=inv_freq,
        conv_wq=gdn_conv_w[l][:, 0:512], conv_wk=gdn_conv_w[l][:, 512:1024], conv_wv=gdn_conv_w[l][:, 1024:1536],
        gdn_a_log=gdn_a_log[l].reshape(1, -1), gdn_dt_bias=gdn_dt_bias[l].reshape(1, -1),
        gdn_o_gain=gdn_o_gain[l].reshape(1, -1),
        moba_qn_gain=jnp.tile(moba_qn_gain[l], HEADS).reshape(1, -1),
        moba_kn_gain=jnp.tile(moba_kn_gain[l], HEADS).reshape(1, -1),
        head_sum=head_sum, head_expand=head_sum.T,
        w_branch=w_branch[l].astype(BF16), w_out=w_out[l].astype(BF16),
        peer_w_query=peer_w_query[l].astype(BF16),
        peer_sub_keys=peer_sub_keys[l].reshape(2 * PEER_HEADS, PEER_NKEYS, -1),
    )


def kernel(x, c, positions, w_mod, b_mod, w_in, mla_q_gain, mla_w_uq, mla_kv_gain, mla_w_ukv, mla_qn_gain, mla_kn_gain, gdn_conv_w, gdn_a_log, gdn_dt_bias, gdn_o_gain, moba_qn_gain, moba_kn_gain, w_branch, w_out, peer_w_query, peer_sub_keys, peer_u, peer_v):
    bsz, seq, d = x.shape
    t = bsz * seq
    depth = w_mod.shape[0]
    xt = x.reshape(t, d)
    pos = positions.reshape(t, 1)
    mod = _mod_all(c, w_mod, b_mod)
    for l in range(depth):
        lw = _prep_layer_weights(l, w_in, mla_q_gain, mla_w_uq, mla_kv_gain, mla_w_ukv, mla_qn_gain, mla_kn_gain,
                                 gdn_conv_w, gdn_a_log, gdn_dt_bias, gdn_o_gain, moba_qn_gain, moba_kn_gain,
                                 w_branch, w_out, peer_w_query, peer_sub_keys)
        sh_a, sc_a, g_a, sh_f, sc_f, g_f = [mod[l, :, i * d:(i + 1) * d] for i in range(6)]
        proj, _ = _modulate_matmul(xt, sh_a, sc_a, lw["w_in"], seq)
        q, k, v = _mla_prep(proj, pos, lw)
        o_mla = _causal_attention(q, k, v, bsz, seq)
        q, k, v = _moba_prep(proj, lw, seq)
        o_moba = _causal_attention(q, k, v, bsz, seq)
        qn, kn, vv, gb = _gdn_prep(proj, lw, seq)
        o_gdn = _gdn_scan(qn, kn, vv, gb, proj, lw, bsz, seq)
        xt = _merge(o_mla, o_gdn, o_moba, proj, xt, g_a, lw, seq)
        qry, h2 = _modulate_matmul(xt, sh_f, sc_f, lw["peer_w_query"], seq)
        ids, gates = _peer_select(qry, lw["peer_sub_keys"])
        uv = jnp.concatenate([peer_u[l], peer_v[l]], axis=1)
        xt = _peer_apply(ids, gates, h2, xt, g_f, uv, seq)
    return xt.reshape(bsz, seq, d)
```

```python
import functools

import jax
import jax.numpy as jnp
from jax import lax
from jax.experimental import pallas as pl
from jax.experimental.pallas import tpu as pltpu

F32 = jnp.float32
BF16 = jnp.bfloat16
HI = lax.Precision.HIGHEST

D_MODEL = 1024
MLA_HEADS = 8
MLA_Q_RANK = 256
MLA_KV_RANK = 128
MLA_NOPE = 64
MLA_ROPE = 32
MLA_V = 64
MLA_QK = MLA_NOPE + MLA_ROPE
ROPE_THETA = 10000.0
HEADS = 8
HEAD_PAD = 128
GDN_DK = 64
GDN_DV = 64
GDN_CONV = 4
GDN_CHUNK = 64
MOBA_DH = 64
MOBA_BLOCK = 256
MOBA_TOPK = 3
N_BRANCH = 3
BRANCH_W = 512
PEER_HEADS = 8
PEER_NKEYS = 128
PEER_TOPK = 16
PEER_SEL = PEER_HEADS * PEER_TOPK
NORM_EPS = 1e-6
NEG_INF = -1e30

COL_LAT = 0
COL_GQ = 512
COL_MV = 2560
COL_MQ = 3072
COL_GATE = 4096
IN_COLS_PAD = 7168

VMEM_LIMIT = 56 * 1024 * 1024


def _cp(sem, vmem=None):
    return pltpu.CompilerParams(dimension_semantics=sem, vmem_limit_bytes=vmem or VMEM_LIMIT)


def _nt(a, b, precision=None):
    return lax.dot_general(a, b, (((1,), (1,)), ((), ())), precision=precision, preferred_element_type=F32)


def _tn(a, b, precision=None):
    return lax.dot_general(a, b, (((0,), (0,)), ((), ())), precision=precision, preferred_element_type=F32)


def _mm(a, b, precision=None):
    return jnp.dot(a, b, precision=precision, preferred_element_type=F32)


def _silu(x):
    return x * jax.nn.sigmoid(x)


def _mod_kernel(c_ref, w_ref, b_ref, o_ref):
    c = c_ref[...]
    o_ref[0] = _mm(_silu(c), w_ref[0], HI) + b_ref[0]


def _mod_all(c, w_mod, b_mod):
    depth, d, n = w_mod.shape
    bsz = c.shape[0]
    tn = 1024
    return pl.pallas_call(
        _mod_kernel,
        grid=(depth, n // tn),
        in_specs=[pl.BlockSpec((bsz, d), lambda l, j: (0, 0)),
                  pl.BlockSpec((1, d, tn), lambda l, j: (l, 0, j)),
                  pl.BlockSpec((1, 1, tn), lambda l, j: (l, 0, j))],
        out_specs=pl.BlockSpec((1, bsz, tn), lambda l, j: (l, 0, j)),
        out_shape=jax.ShapeDtypeStruct((depth, bsz, n), F32),
        compiler_params=_cp(("arbitrary", "arbitrary")),
        name="adaln_mod",
    )(c, w_mod, b_mod.reshape(depth, 1, n))


def _modmm_kernel(x_ref, sh_ref, sc_ref, w_ref, wlo_ref, o_ref, h_ref, hb_scr, hlo_scr, *, lo_first, lo_tiles):
    j = pl.program_id(1)

    @pl.when(j == 0)
    def _():
        x = x_ref[...]
        h = x * lax.rsqrt(jnp.mean(x * x, axis=-1, keepdims=True) + NORM_EPS)
        h = h * (1.0 + sc_ref[0]) + sh_ref[0]
        h_ref[...] = h
        hb = h.astype(BF16)
        hb_scr[...] = hb
        hlo_scr[...] = (h - hb.astype(F32)).astype(BF16)

    precise = (j >= lo_first) & (j < lo_first + lo_tiles)

    @pl.when(precise)
    def _():
        o_ref[...] = (_mm(hb_scr[...], w_ref[...]) + _mm(hb_scr[...], wlo_ref[...])
                      + _mm(hlo_scr[...], w_ref[...]))

    @pl.when(jnp.logical_not(precise))
    def _():
        o_ref[...] = _mm(hb_scr[...], w_ref[...])


def _modulate_matmul(x, shift, scale, w_bf16, w_lo, lo_first, seq, tm=512, tn=1024):
    t, d = x.shape
    n = w_bf16.shape[1]
    lo_tiles = w_lo.shape[1] // tn
    bsz = shift.shape[0]
    bidx = lambda i, j: ((i * tm) // seq, 0, 0)
    return pl.pallas_call(
        functools.partial(_modmm_kernel, lo_first=lo_first, lo_tiles=lo_tiles),
        grid=(t // tm, n // tn),
        in_specs=[pl.BlockSpec((tm, d), lambda i, j: (i, 0)),
                  pl.BlockSpec((1, 1, d), bidx),
                  pl.BlockSpec((1, 1, d), bidx),
                  pl.BlockSpec((d, tn), lambda i, j: (0, j)),
                  pl.BlockSpec((d, tn), lambda i, j: (0, jnp.clip(j - lo_first, 0, lo_tiles - 1)))],
        out_specs=[pl.BlockSpec((tm, tn), lambda i, j: (i, j)),
                   pl.BlockSpec((tm, d), lambda i, j: (i, 0))],
        out_shape=[jax.ShapeDtypeStruct((t, n), F32), jax.ShapeDtypeStruct((t, d), F32)],
        scratch_shapes=[pltpu.VMEM((tm, d), BF16), pltpu.VMEM((tm, d), BF16)],
        compiler_params=_cp(("arbitrary", "arbitrary")),
        name="modulate_matmul",
    )(x, shift.reshape(bsz, 1, d), scale.reshape(bsz, 1, d), w_bf16, w_lo)


def _mla_prep_kernel(p_ref, pos_ref, qg_ref, wuq_ref, kvg_ref, wuk_ref, wuv_ref, qng_ref, kng_ref, invf_ref,
                     q_out, k_out, v_out):
    tm = p_ref.shape[0]
    cq = p_ref[:, 0:MLA_Q_RANK]
    ckv = p_ref[:, MLA_Q_RANK:MLA_Q_RANK + MLA_KV_RANK]
    misc = p_ref[:, MLA_Q_RANK + MLA_KV_RANK:MLA_Q_RANK + MLA_KV_RANK + 128]

    def rms(v, n):
        return v * lax.rsqrt(jnp.sum(v * v, axis=-1, keepdims=True) * (1.0 / n) + NORM_EPS)

    qn = (rms(cq, MLA_Q_RANK) * qg_ref[...]).astype(BF16)
    q_all = _mm(qn, wuq_ref[...])
    kvn = (rms(ckv, MLA_KV_RANK) * kvg_ref[...]).astype(BF16)
    k_all = _mm(kvn, wuk_ref[...])
    v_out[...] = _mm(kvn, wuv_ref[...]).astype(BF16)

    lane = lax.broadcasted_iota(jnp.int32, (tm, HEAD_PAD), 1)
    in_rope = (lane >= MLA_NOPE) & (lane < MLA_QK)
    k_rope = jnp.where(in_rope, pltpu.roll(misc, MLA_NOPE, 1), 0.0)
    ang = pos_ref[...].astype(F32) * invf_ref[...]
    cos = jnp.cos(ang)
    sin = jnp.sin(ang)
    half = MLA_ROPE // 2
    c_tab = jnp.where(lane < MLA_NOPE, 1.0, jnp.where(in_rope, cos, 0.0))
    s_lo = jnp.where(in_rope & (lane < MLA_NOPE + half), -sin, 0.0)
    s_hi = jnp.where(in_rope & (lane >= MLA_NOPE + half), sin, 0.0)

    def finish(xh, gain):
        xh = xh * lax.rsqrt(jnp.sum(xh * xh, axis=-1, keepdims=True) * (1.0 / MLA_QK) + NORM_EPS) * gain
        return xh * c_tab + pltpu.roll(xh, HEAD_PAD - half, 1) * s_lo + pltpu.roll(xh, half, 1) * s_hi

    scale = MLA_QK ** -0.5
    for h in range(MLA_HEADS):
        sl = slice(h * HEAD_PAD, (h + 1) * HEAD_PAD)
        q_out[:, sl] = (finish(q_all[:, sl], qng_ref[...]) * scale).astype(BF16)
        k_out[:, sl] = finish(k_all[:, sl] + k_rope, kng_ref[...]).astype(BF16)


def _mla_prep(proj, pos, lw, tm=256):
    t = proj.shape[0]
    full = lambda shp: pl.BlockSpec(shp, lambda i: (0,) * len(shp))
    return pl.pallas_call(
        _mla_prep_kernel,
        grid=(t // tm,),
        in_specs=[pl.BlockSpec((tm, 512), lambda i: (i, 0)),
                  pl.BlockSpec((tm, 1), lambda i: (i, 0)),
                  full((1, MLA_Q_RANK)), full((MLA_Q_RANK, HEADS * HEAD_PAD)),
                  full((1, MLA_KV_RANK)), full((MLA_KV_RANK, HEADS * HEAD_PAD)),
                  full((MLA_KV_RANK, HEADS * MLA_V)),
                  full((1, HEAD_PAD)), full((1, HEAD_PAD)), full((1, HEAD_PAD))],
        out_specs=[pl.BlockSpec((tm, HEADS * HEAD_PAD), lambda i: (i, 0)),
                   pl.BlockSpec((tm, HEADS * HEAD_PAD), lambda i: (i, 0)),
                   pl.BlockSpec((tm, HEADS * MLA_V), lambda i: (i, 0))],
        out_shape=[jax.ShapeDtypeStruct((t, HEADS * HEAD_PAD), BF16),
                   jax.ShapeDtypeStruct((t, HEADS * HEAD_PAD), BF16),
                   jax.ShapeDtypeStruct((t, HEADS * MLA_V), BF16)],
        compiler_params=_cp(("arbitrary",)),
        name="mla_prep",
    )(proj, pos, lw["mla_q_gain"], lw["mla_w_uq"], lw["mla_kv_gain"], lw["mla_w_uk"], lw["mla_w_uv"],
      lw["mla_qn_gain"], lw["mla_kn_gain"], lw["rope_inv_freq"])


def _attn_kernel(q_ref, k_ref, v_ref, o_ref, *, tq):
    i = pl.program_id(2)
    row = lax.broadcasted_iota(jnp.int32, (tq, tq), 0)
    col = lax.broadcasted_iota(jnp.int32, (tq, tq), 1)
    outs = []
    for hh in range(2):
        q = q_ref[:, hh * HEAD_PAD:(hh + 1) * HEAD_PAD]

        def step(j, carry, masked, hh=hh, q=q):
            m, l, acc = carry
            start = pl.multiple_of(j * tq, tq)
            kk = k_ref[pl.ds(start, tq), hh * HEAD_PAD:(hh + 1) * HEAD_PAD]
            vv = v_ref[pl.ds(start, tq), :]
            s = _nt(q, kk)
            if masked:
                s = jnp.where(col <= row, s, NEG_INF)
            m_new = jnp.maximum(m, jnp.max(s, axis=-1, keepdims=True))
            p = jnp.exp(s - m_new)
            alpha = jnp.exp(m - m_new)
            l = alpha * l + jnp.sum(p, axis=-1, keepdims=True)
            acc = alpha * acc + _mm(p.astype(BF16), vv)
            return m_new, l, acc

        init = (jnp.full((tq, 1), -jnp.inf, F32), jnp.zeros((tq, 1), F32), jnp.zeros((tq, 2 * MLA_V), F32))
        carry = lax.fori_loop(0, i, functools.partial(step, masked=False), init)
        m, l, acc = step(i, carry, True)
        outs.append(acc / l)
    lane = lax.broadcasted_iota(jnp.int32, (tq, 2 * MLA_V), 1)
    o_ref[...] = jnp.where(lane < MLA_V, outs[0], outs[1])


def _causal_attention(q, k, v, bsz, seq, tq=512):
    t = q.shape[0]
    nq = seq // tq
    return pl.pallas_call(
        functools.partial(_attn_kernel, tq=tq),
        grid=(bsz, HEADS // 2, nq),
        in_specs=[pl.BlockSpec((tq, 2 * HEAD_PAD), lambda b, hp, i: (b * nq + i, hp)),
                  pl.BlockSpec((seq, 2 * HEAD_PAD), lambda b, hp, i: (b, hp)),
                  pl.BlockSpec((seq, 2 * MLA_V), lambda b, hp, i: (b, hp))],
        out_specs=pl.BlockSpec((tq, 2 * MLA_V), lambda b, hp, i: (b * nq + i, hp)),
        out_shape=jax.ShapeDtypeStruct((t, HEADS * MLA_V), F32),
        compiler_params=_cp(("arbitrary", "arbitrary", "arbitrary")),
        name="causal_attention",
    )(q, k, v)


def _moba_prep_kernel(mq_ref, mk_ref, mv_ref, qg_ref, kg_ref, e_ref, et_ref, q_out, k_out, v_out, kmean_scr, *, nb):
    tm = mq_ref.shape[0]
    n = pl.program_id(0) % nb

    @pl.when(n == 0)
    def _():
        kmean_scr[...] = jnp.zeros_like(kmean_scr)

    def headnorm(x, gain):
        ss = _mm(x * x, e_ref[...], HI)
        inv = lax.rsqrt(ss * (1.0 / MOBA_DH) + NORM_EPS)
        return x * _mm(inv, et_ref[...], HI) * gain

    qn = headnorm(mq_ref[...], qg_ref[...])
    kn = headnorm(mk_ref[...], kg_ref[...])
    v_out[...] = mv_ref[...].astype(BF16)
    kmean_scr[pl.ds(n, 1), :] = jnp.mean(kn, axis=0, keepdims=True)
    km = kmean_scr[...]

    lane = lax.broadcasted_iota(jnp.int32, (tm, nb), 1)
    zpad = jnp.zeros((tm, HEAD_PAD - MOBA_DH - nb), F32)
    onehot = jnp.where(lane == n, 1.0, 0.0)
    for h in range(HEADS):
        sl = slice(h * MOBA_DH, (h + 1) * MOBA_DH)
        gate = _nt(qn[:, sl], km[:, sl], HI)
        gate = jnp.where(lane < n, gate, -jnp.inf)
        pen = jnp.full((tm, nb), NEG_INF, F32)
        for r in range(MOBA_TOPK):
            mx = jnp.max(gate, axis=-1, keepdims=True)
            idx = jnp.min(jnp.where(gate == mx, lane, nb), axis=-1, keepdims=True)
            hit = (lane == idx) & (r < n)
            pen = jnp.where(hit, 0.0, pen)
            gate = jnp.where(lane == idx, -jnp.inf, gate)
        pen = jnp.where(lane == n, 0.0, pen)
        osl = slice(h * HEAD_PAD, (h + 1) * HEAD_PAD)
        q_out[:, osl] = jnp.concatenate([qn[:, sl] * (MOBA_DH ** -0.5), pen, zpad], axis=1).astype(BF16)
        k_out[:, osl] = jnp.concatenate([kn[:, sl], onehot, zpad], axis=1).astype(BF16)


def _moba_prep(proj, lw, seq):
    t = proj.shape[0]
    tm = MOBA_BLOCK
    nb = seq // tm
    full = lambda shp: pl.BlockSpec(shp, lambda i: (0,) * len(shp))
    c0 = COL_MQ // 512
    return pl.pallas_call(
        functools.partial(_moba_prep_kernel, nb=nb),
        grid=(t // tm,),
        in_specs=[pl.BlockSpec((tm, 512), lambda i: (i, c0)),
                  pl.BlockSpec((tm, 512), lambda i: (i, c0 + 1)),
                  pl.BlockSpec((tm, 512), lambda i: (i, COL_MV // 512)),
                  full((1, 512)), full((1, 512)), full((512, HEADS)), full((HEADS, 512))],
        out_specs=[pl.BlockSpec((tm, HEADS * HEAD_PAD), lambda i: (i, 0)),
                   pl.BlockSpec((tm, HEADS * HEAD_PAD), lambda i: (i, 0)),
                   pl.BlockSpec((tm, HEADS * MOBA_DH), lambda i: (i, 0))],
        out_shape=[jax.ShapeDtypeStruct((t, HEADS * HEAD_PAD), BF16),
                   jax.ShapeDtypeStruct((t, HEADS * HEAD_PAD), BF16),
                   jax.ShapeDtypeStruct((t, HEADS * MOBA_DH), BF16)],
        scratch_shapes=[pltpu.VMEM((nb, HEADS * MOBA_DH), F32)],
        compiler_params=_cp(("arbitrary",)),
        name="moba_prep",
    )(proj, proj, proj, lw["moba_qn_gain"], lw["moba_kn_gain"], lw["head_sum"], lw["head_expand"])


def _gdn_prep_kernel(gq_ref, gk_ref, gv_ref, misc_ref, wq_ref, wk_ref, wv_ref, alog_ref, dtb_ref, e_ref, et_ref,
                     q_out, k_out, v_out, gb_out, ext_scr, *, tiles_per_seq):
    tm = gq_ref.shape[0]
    pad = 8

    @pl.when(pl.program_id(0) % tiles_per_seq == 0)
    def _():
        ext_scr[:, 0:pad, :] = jnp.zeros((3, pad, ext_scr.shape[2]), F32)

    def conv_silu(s, x_ref, w_ref):
        ext_scr[s, pad:pad + tm, :] = x_ref[...]
        y = jnp.zeros(x_ref.shape, F32)
        for i in range(GDN_CONV):
            y = y + ext_scr[s, pl.ds(pad - (GDN_CONV - 1) + i, tm), :] * w_ref[i:i + 1, :]
        ext_scr[s, 0:pad, :] = ext_scr[s, tm:tm + pad, :]
        return _silu(y)

    def l2n(x):
        ss = _mm(x * x, e_ref[...], HI)
        return x * _mm(lax.rsqrt(ss + NORM_EPS), et_ref[...], HI)

    q_out[...] = l2n(conv_silu(0, gq_ref, wq_ref)) * (GDN_DK ** -0.5)
    k_out[...] = l2n(conv_silu(1, gk_ref, wk_ref))
    v_out[...] = conv_silu(2, gv_ref, wv_ref)
    misc = misc_ref[...]
    a = misc[:, MLA_ROPE:MLA_ROPE + HEADS]
    b = misc[:, MLA_ROPE + HEADS:MLA_ROPE + 2 * HEADS]
    g = -jnp.exp(alog_ref[...]) * jax.nn.softplus(a + dtb_ref[...])
    beta = jax.nn.sigmoid(b)
    gb_out[...] = jnp.concatenate([g, beta, jnp.zeros((tm, 128 - 2 * HEADS), F32)], axis=1)


def _gdn_prep(proj, lw, seq, tm=256):
    t = proj.shape[0]
    full = lambda shp: pl.BlockSpec(shp, lambda i: (0,) * len(shp))
    c0 = COL_GQ // 512
    tok = lambda w: pl.BlockSpec((tm, w), lambda i: (i, 0))
    return pl.pallas_call(
        functools.partial(_gdn_prep_kernel, tiles_per_seq=seq // tm),
        grid=(t // tm,),
        in_specs=[pl.BlockSpec((tm, 512), lambda i: (i, c0)),
                  pl.BlockSpec((tm, 512), lambda i: (i, c0 + 1)),
                  pl.BlockSpec((tm, 512), lambda i: (i, c0 + 2)),
                  pl.BlockSpec((tm, 128), lambda i: (i, 3)),
                  full((GDN_CONV, 512)), full((GDN_CONV, 512)), full((GDN_CONV, 512)),
                  full((1, HEADS)), full((1, HEADS)), full((512, HEADS)), full((HEADS, 512))],
        out_specs=[tok(512), tok(512), tok(512), tok(128)],
        out_shape=[jax.ShapeDtypeStruct((t, 512), F32)] * 3 + [jax.ShapeDtypeStruct((t, 128), F32)],
        scratch_shapes=[pltpu.VMEM((3, tm + 8, 512), F32)],
        compiler_params=_cp(("arbitrary",)),
        name="gdn_prep",
    )(proj, proj, proj, proj, lw["conv_wq"], lw["conv_wk"], lw["conv_wv"], lw["gdn_a_log"], lw["gdn_dt_bias"],
      lw["head_sum"], lw["head_expand"])


def _gdn_scan_kernel(q_ref, k_ref, v_ref, gb_ref, z_ref, og_ref, o_ref, s_scr, *, chunks):
    c = GDN_CHUNK

    @pl.when(pl.program_id(1) == 0)
    def _():
        s_scr[...] = jnp.zeros_like(s_scr)

    ri = lax.broadcasted_iota(jnp.int32, (c, c), 0)
    ci = lax.broadcasted_iota(jnp.int32, (c, c), 1)
    tri = ci <= ri
    strict = ci < ri
    ltri = jnp.where(tri, 1.0, 0.0)
    probs = [(cc, h) for cc in range(chunks) for h in range(HEADS)]
    rows = lambda cc: slice(cc * c, (cc + 1) * c)
    hsl = lambda h: slice(h * GDN_DK, (h + 1) * GDN_DK)
    gbs = [gb_ref[rows(cc), :] for cc in range(chunks)]
    gcs = [_mm(ltri, gb[:, 0:HEADS], HI) for gb in gbs]
    gcts = [g.T for g in gcs]
    q = [q_ref[rows(cc), hsl(h)] for cc, h in probs]
    k = [k_ref[rows(cc), hsl(h)] for cc, h in probs]
    v = [v_ref[rows(cc), hsl(h)] for cc, h in probs]
    gcol = [gcs[cc][:, h:h + 1] for cc, h in probs]
    grow = [gcts[cc][h:h + 1, :] for cc, h in probs]
    glast = [gcs[cc][c - 1:c, h:h + 1] for cc, h in probs]
    bcol = [gbs[cc][:, HEADS + h:HEADS + h + 1] for cc, h in probs]
    n = len(probs)
    egc = [jnp.exp(g) for g in gcol]
    decay = [jnp.where(tri, jnp.exp(jnp.where(tri, gcol[i] - grow[i], 0.0)), 0.0) for i in range(n)]
    kb = [k[i] * bcol[i] for i in range(n)]
    k16 = [a.astype(BF16) for a in k]
    kb16 = [a.astype(BF16) for a in kb]
    a_neg = [jnp.where(strict, -(_nt(kb16[i], k16[i]) * decay[i]), 0.0) for i in range(n)]
    x = [jnp.concatenate([v[i] * bcol[i], kb[i] * egc[i]], axis=1) for i in range(n)]
    def mm_split_rhs(a16_, rhs):
        hi = rhs.astype(BF16)
        lo = (rhs - hi.astype(F32)).astype(BF16)
        return _mm(a16_, hi) + _mm(a16_, lo)

    for lvl in range(6):
        a16 = [a.astype(BF16) for a in a_neg]
        if lvl < 5:
            r = [mm_split_rhs(a16[i], jnp.concatenate([a_neg[i], x[i]], axis=1)) for i in range(n)]
            a_neg = [ri_[:, 0:c] for ri_ in r]
            x = [x[i] + r[i][:, c:] for i in range(n)]
        else:
            x = [x[i] + mm_split_rhs(a16[i], x[i]) for i in range(n)]
    u = [xi[:, 0:GDN_DV] for xi in x]
    w16 = [xi[:, GDN_DV:].astype(BF16) for xi in x]
    attn16 = [(_nt(q[i].astype(BF16), k16[i]) * decay[i]).astype(BF16) for i in range(n)]
    qe16 = [(q[i] * egc[i]).astype(BF16) for i in range(n)]
    kd16 = [(k[i] * jnp.exp(glast[i] - gcol[i])).astype(BF16) for i in range(n)]
    eglast = [jnp.exp(g) for g in glast]
    state = [s_scr[h] for h in range(HEADS)]
    for cc in range(chunks):
        ids = [cc * HEADS + h for h in range(HEADS)]
        s16 = [s.astype(BF16) for s in state]
        v_new = [u[i] - _mm(w16[i], s16[h]) for h, i in enumerate(ids)]
        vn16 = [a.astype(BF16) for a in v_new]
        o = [_mm(qe16[i], s16[h]) + _mm(attn16[i], vn16[h]) for h, i in enumerate(ids)]
        state = [state[h] * eglast[i] + _tn(kd16[i], vn16[h]) for h, i in enumerate(ids)]
        o = [oh * lax.rsqrt(jnp.mean(oh * oh, axis=-1, keepdims=True) + NORM_EPS) * og_ref[...] for oh in o]
        o_ref[rows(cc), :] = jnp.concatenate(o, axis=1) * _silu(z_ref[rows(cc), :])
    for h in range(HEADS):
        s_scr[h] = state[h]


def _gdn_scan(qn, kn, vv, gb, proj, lw, bsz, seq, chunks=4):
    t = qn.shape[0]
    ct = chunks * GDN_CHUNK
    ns = seq // ct
    tok = lambda w: pl.BlockSpec((ct, w), lambda b, i: (b * ns + i, 0))
    return pl.pallas_call(
        functools.partial(_gdn_scan_kernel, chunks=chunks),
        grid=(bsz, ns),
        in_specs=[tok(512), tok(512), tok(512), tok(128),
                  pl.BlockSpec((ct, 512), lambda b, i: (b * ns + i, COL_GQ // 512 + 3)),
                  pl.BlockSpec((1, GDN_DV), lambda b, i: (0, 0))],
        out_specs=tok(512),
        out_shape=jax.ShapeDtypeStruct((t, 512), F32),
        scratch_shapes=[pltpu.VMEM((HEADS, GDN_DK, GDN_DV), F32)],
        compiler_params=_cp(("arbitrary", "arbitrary")),
        name="gdn_scan",
    )(qn, kn, vv, gb, proj, lw["gdn_o_gain"])


def _merge_kernel(oa_ref, ob_ref, oc_ref, g0_ref, g1_ref, g2_ref, x_ref, ga_ref, wb_ref, wo_ref, o_ref):
    y = None
    for o_n, g_n, n in ((oa_ref, g0_ref, 0), (ob_ref, g1_ref, 1), (oc_ref, g2_ref, 2)):
        term = jax.nn.sigmoid(g_n[...]) * _mm(o_n[...].astype(BF16), wb_ref[n])
        y = term if y is None else y + term
    o_ref[...] = x_ref[...] + ga_ref[0] * _mm(y.astype(BF16), wo_ref[...])


def _merge(o_mla, o_gdn, o_moba, proj, x, g_a, lw, seq, tm=512):
    t, d = x.shape
    bsz = g_a.shape[0]
    tok = lambda w: pl.BlockSpec((tm, w), lambda i: (i, 0))
    gcol = COL_GATE // d
    return pl.pallas_call(
        _merge_kernel,
        grid=(t // tm,),
        in_specs=[tok(BRANCH_W), tok(BRANCH_W), tok(BRANCH_W),
                  pl.BlockSpec((tm, d), lambda i: (i, gcol)),
                  pl.BlockSpec((tm, d), lambda i: (i, gcol + 1)),
                  pl.BlockSpec((tm, d), lambda i: (i, gcol + 2)),
                  tok(d),
                  pl.BlockSpec((1, 1, d), lambda i: ((i * tm) // seq, 0, 0)),
                  pl.BlockSpec((N_BRANCH, BRANCH_W, d), lambda i: (0, 0, 0)),
                  pl.BlockSpec((d, d), lambda i: (0, 0))],
        out_specs=tok(d),
        out_shape=jax.ShapeDtypeStruct((t, d), F32),
        compiler_params=_cp(("arbitrary",)),
        name="branch_merge",
    )(o_mla, o_gdn, o_moba, proj, proj, proj, x, g_a.reshape(bsz, 1, d), lw["w_branch"], lw["w_out"])


def _topk_rows(s, k):
    n = s.shape[0]
    ri = lax.broadcasted_iota(jnp.int32, s.shape, 0)
    vals, idxs = [], []
    for _ in range(k):
        mx = jnp.max(s, axis=0, keepdims=True)
        idx = jnp.min(jnp.where(s == mx, ri, n), axis=0, keepdims=True)
        vals.append(mx)
        idxs.append(idx)
        s = jnp.where(ri == idx, -jnp.inf, s)
    return jnp.concatenate(vals, axis=0), jnp.concatenate(idxs, axis=0)


def _peer_select_kernel(qry_ref, keys_ref, ids_out, gates_out):
    kk = PEER_TOPK
    ids_rows, gate_rows = [], []
    for h in range(PEER_HEADS):
        halves = []
        for p in range(2):
            g = h * 2 + p
            s_t = _nt(keys_ref[g], qry_ref[:, g * 128:(g + 1) * 128], HI)
            halves.append(_topk_rows(s_t, kk))
        (v1, i1), (v2, i2) = halves
        cand, a_of, b_of = [], [], []
        for a in range(kk):
            nb = kk // (a + 1)
            cand.append(v1[a:a + 1, :] + v2[0:nb, :])
            a_of += [a] * nb
            b_of += list(range(nb))
        cand = jnp.concatenate(cand, axis=0)
        top_s, top_r = _topk_rows(cand, kk)
        a_sel = jnp.zeros_like(top_r)
        b_sel = jnp.zeros_like(top_r)
        for r, (a, b) in enumerate(zip(a_of, b_of)):
            hit = top_r == r
            a_sel = jnp.where(hit, a, a_sel)
            b_sel = jnp.where(hit, b, b_sel)
        e1 = jnp.zeros_like(top_r)
        e2 = jnp.zeros_like(top_r)
        for a in range(kk):
            e1 = jnp.where(a_sel == a, i1[a:a + 1, :], e1)
            e2 = jnp.where(b_sel == a, i2[a:a + 1, :], e2)
        ids_rows.append(e1 * PEER_NKEYS + e2)
        ex = jnp.exp(top_s - top_s[0:1, :])
        gate_rows.append(ex / jnp.sum(ex, axis=0, keepdims=True))
    ids_out[...] = jnp.concatenate(ids_rows, axis=0).T
    gates_out[...] = jnp.concatenate(gate_rows, axis=0).T


def _peer_select(qry, sub_keys, tm=256):
    t = qry.shape[0]
    return pl.pallas_call(
        _peer_select_kernel,
        grid=(t // tm,),
        in_specs=[pl.BlockSpec((tm, 2 * PEER_HEADS * 128), lambda i: (i, 0)),
                  pl.BlockSpec((2 * PEER_HEADS, PEER_NKEYS, 128), lambda i: (0, 0, 0))],
        out_specs=[pl.BlockSpec((tm, PEER_SEL), lambda i: (i, 0)),
                   pl.BlockSpec((tm, PEER_SEL), lambda i: (i, 0))],
        out_shape=[jax.ShapeDtypeStruct((t, PEER_SEL), jnp.int32), jax.ShapeDtypeStruct((t, PEER_SEL), F32)],
        compiler_params=_cp(("arbitrary",)),
        name="peer_select",
    )(qry, sub_keys)


def _peer_apply_kernel(ids_ref, gates_ref, h_ref, x_ref, gf_ref, uv_hbm, o_ref, buf0, buf1, sems, *, tt):
    s = pl.program_id(0)
    ns = pl.num_programs(0) - 1
    slot = s % 2
    buf = (buf0, buf1)

    def start_rows(dst_slot):
        for t in range(tt):
            for e in range(PEER_SEL):
                pltpu.make_async_copy(uv_hbm.at[ids_ref[t, e]], buf[dst_slot].at[t, pl.ds(e, 1)],
                                      sems.at[dst_slot]).start(priority=e % 2)

    def wait_rows(src_slot):
        pltpu.make_async_copy(buf[1 - src_slot], buf[src_slot], sems.at[src_slot]).wait()

    def reduce_tile(src_slot):
        d = h_ref.shape[1]
        hb = h_ref[...]
        ub = buf[src_slot][:, :, 0:d]
        act = jnp.sum(ub * hb[:, None, :], axis=-1)
        gel = 0.5 * act * (1.0 + lax.erf(act * (2.0 ** -0.5)))
        wgt = gates_ref[...] * gel
        vb = buf[src_slot][:, :, d:2 * d]
        out = jnp.sum(wgt[:, :, None] * vb, axis=1)
        o_ref[...] = x_ref[...] + gf_ref[0] * out

    def steady(parity):
        wait_rows(1 - parity)
        start_rows(parity)
        reduce_tile(1 - parity)

    def last(parity):
        wait_rows(1 - parity)
        reduce_tile(1 - parity)

    for parity in range(2):
        pl.when((s >= 1) & (s < ns) & (slot == parity))(functools.partial(steady, parity))
    pl.when(s == 0)(functools.partial(start_rows, 0))
    for parity in range(2):
        pl.when((s == ns) & (slot == parity))(functools.partial(last, parity))


def _peer_apply(ids, gates, h2, x, g_f, uv, seq, tt=16):
    t, d = x.shape
    bsz = g_f.shape[0]
    ns = t // tt
    cur = lambda i: jnp.maximum(i - 1, 0)
    tok = lambda w: pl.BlockSpec((tt, w), lambda i: (cur(i), 0))
    return pl.pallas_call(
        functools.partial(_peer_apply_kernel, tt=tt),
        grid=(ns + 1,),
        in_specs=[pl.BlockSpec((tt, PEER_SEL), lambda i: (jnp.minimum(i, ns - 1), 0), memory_space=pltpu.SMEM),
                  tok(PEER_SEL), tok(d), tok(d),
                  pl.BlockSpec((1, 1, d), lambda i: ((cur(i) * tt) // seq, 0, 0)),
                  pl.BlockSpec(memory_space=pl.ANY)],
        out_specs=tok(d),
        out_shape=jax.ShapeDtypeStruct((t, d), F32),
        scratch_shapes=[pltpu.VMEM((tt, PEER_SEL, 2 * d), F32),
                        pltpu.VMEM((tt, PEER_SEL, 2 * d), F32),
                        pltpu.SemaphoreType.DMA((2,))],
        compiler_params=_cp(("arbitrary",)),
        name="peer_apply",
    )(ids, gates, h2, x, g_f.reshape(bsz, 1, d), uv)


def _prep_layer_weights(l, w_in, mla_q_gain, mla_w_uq, mla_kv_gain, mla_w_ukv, mla_qn_gain, mla_kn_gain,
                        gdn_conv_w, gdn_a_log, gdn_dt_bias, gdn_o_gain, moba_qn_gain, moba_kn_gain,
                        w_branch, w_out, peer_w_query, peer_sub_keys):
    d = D_MODEL
    w = w_in[l]
    o = 0
    parts = {}
    for name, width in (("cq", 256), ("ckv", 128), ("kr", 32), ("gq", 512), ("gk", 512), ("gv", 512), ("gz", 512),
                        ("ga", 8), ("gb", 8), ("mq", 512), ("mk", 512), ("mv", 512), ("gate", 3072)):
        parts[name] = w[:, o:o + width]
        o += width
    w_in_p = jnp.concatenate(
        [parts["cq"], parts["ckv"], parts["kr"], parts["ga"], parts["gb"], jnp.zeros((d, 80), F32),
         parts["gq"], parts["gk"], parts["gv"], parts["gz"], parts["mv"], parts["mq"], parts["mk"], parts["gate"]],
        axis=1)
    w_in_hi = w_in_p.astype(BF16)
    sel = slice(COL_MQ, COL_MQ + 1024)
    w_in_lo = (w_in_p[:, sel] - w_in_hi[:, sel].astype(F32)).astype(BF16)
    wq_hi = peer_w_query[l].astype(BF16)
    wq_lo = (peer_w_query[l] - wq_hi.astype(F32)).astype(BF16)

    def pad_heads(m, width):
        r = m.shape[0]
        return jnp.pad(m.reshape(r, HEADS, width), ((0, 0), (0, 0), (0, HEAD_PAD - width))).reshape(r, HEADS * HEAD_PAD)

    ukv = mla_w_ukv[l].reshape(MLA_KV_RANK, HEADS, MLA_NOPE + MLA_V)
    pad1 = lambda g, n: jnp.pad(g, (0, n - g.shape[0])).reshape(1, n)
    lane = jnp.arange(HEAD_PAD)
    half = MLA_ROPE // 2
    inv_freq = jnp.where((lane >= MLA_NOPE) & (lane < MLA_QK),
                         ROPE_THETA ** (-((lane - MLA_NOPE) % half).astype(F32) / half), 0.0).reshape(1, HEAD_PAD)
    head_sum = jnp.repeat(jnp.eye(HEADS, dtype=F32), 64, axis=0)
    return dict(
        w_in=w_in_hi, w_in_lo=w_in_lo,
        mla_q_gain=mla_q_gain[l].reshape(1, -1),
        mla_w_uq=pad_heads(mla_w_uq[l], MLA_QK).astype(BF16),
        mla_kv_gain=mla_kv_gain[l].reshape(1, -1),
        mla_w_uk=pad_heads(ukv[:, :, :MLA_NOPE].reshape(MLA_KV_RANK, -1), MLA_NOPE).astype(BF16),
        mla_w_uv=ukv[:, :, MLA_NOPE:].reshape(MLA_KV_RANK, -1).astype(BF16),
        mla_qn_gain=pad1(mla_qn_gain[l], HEAD_PAD),
        mla_kn_gain=pad1(mla_kn_gain[l], HEAD_PAD),
        rope_inv_freq=inv_freq,
        conv_wq=gdn_conv_w[l][:, 0:512], conv_wk=gdn_conv_w[l][:, 512:1024], conv_wv=gdn_conv_w[l][:, 1024:1536],
        gdn_a_log=gdn_a_log[l].reshape(1, -1), gdn_dt_bias=gdn_dt_bias[l].reshape(1, -1),
        gdn_o_gain=gdn_o_gain[l].reshape(1, -1),
        moba_qn_gain=jnp.tile(moba_qn_gain[l], HEADS).reshape(1, -1),
        moba_kn_gain=jnp.tile(moba_kn_gain[l], HEADS).reshape(1, -1),
        head_sum=head_sum, head_expand=head_sum.T,
        w_branch=w_branch[l].astype(BF16), w_out=w_out[l].astype(BF16),
        peer_w_query=wq_hi, peer_w_query_lo=wq_lo,
        peer_sub_keys=peer_sub_keys[l].reshape(2 * PEER_HEADS, PEER_NKEYS, -1),
    )


def kernel(x, c, positions, w_mod, b_mod, w_in, mla_q_gain, mla_w_uq, mla_kv_gain, mla_w_ukv, mla_qn_gain, mla_kn_gain, gdn_conv_w, gdn_a_log, gdn_dt_bias, gdn_o_gain, moba_qn_gain, moba_kn_gain, w_branch, w_out, peer_w_query, peer_sub_keys, peer_u, peer_v):
    bsz, seq, d = x.shape
    t = bsz * seq
    depth = w_mod.shape[0]
    xt = x.reshape(t, d)
    pos = positions.reshape(t, 1)
    mod = _mod_all(c, w_mod, b_mod)
    for l in range(depth):
        lw = _prep_layer_weights(l, w_in, mla_q_gain, mla_w_uq, mla_kv_gain, mla_w_ukv, mla_qn_gain, mla_kn_gain,
                                 gdn_conv_w, gdn_a_log, gdn_dt_bias, gdn_o_gain, moba_qn_gain, moba_kn_gain,
                                 w_branch, w_out, peer_w_query, peer_sub_keys)
        sh_a, sc_a, g_a, sh_f, sc_f, g_f = [mod[l, :, i * d:(i + 1) * d] for i in range(6)]
        proj, _ = _modulate_matmul(xt, sh_a, sc_a, lw["w_in"], lw["w_in_lo"], COL_MQ // 1024, seq)
        q, k, v = _mla_prep(proj, pos, lw)
        o_mla = _causal_attention(q, k, v, bsz, seq)
        q, k, v = _moba_prep(proj, lw, seq)
        o_moba = _causal_attention(q, k, v, bsz, seq)
        qn, kn, vv, gb = _gdn_prep(proj, lw, seq)
        o_gdn = _gdn_scan(qn, kn, vv, gb, proj, lw, bsz, seq)
        xt = _merge(o_mla, o_gdn, o_moba, proj, xt, g_a, lw, seq)
        qry, h2 = _modulate_matmul(xt, sh_f, sc_f, lw["peer_w_query"], lw["peer_w_query_lo"], 0, seq)
        ids, gates = _peer_select(qry, lw["peer_sub_keys"])
        uv = jnp.concatenate([peer_u[l], peer_v[l]], axis=1)[:, None, :]
        xt = _peer_apply(ids, gates, h2, xt, g_f, uv, seq)
    return xt.reshape(bsz, seq, d)
```

```python
import functools

import jax
import jax.numpy as jnp
from jax import lax
from jax.experimental import pallas as pl
from jax.experimental.pallas import tpu as pltpu

F32 = jnp.float32
BF16 = jnp.bfloat16
HI = lax.Precision.HIGHEST

D_MODEL = 1024
MLA_HEADS = 8
MLA_Q_RANK = 256
MLA_KV_RANK = 128
MLA_NOPE = 64
MLA_ROPE = 32
MLA_V = 64
MLA_QK = MLA_NOPE + MLA_ROPE
ROPE_THETA = 10000.0
HEADS = 8
HEAD_PAD = 128
GDN_DK = 64
GDN_DV = 64
GDN_CONV = 4
GDN_CHUNK = 64
MOBA_DH = 64
MOBA_BLOCK = 256
ATTN_BLOCK = MOBA_BLOCK
MOBA_TOPK = 3
N_BRANCH = 3
BRANCH_W = 512
PEER_HEADS = 8
PEER_NKEYS = 128
PEER_TOPK = 16
PEER_SEL = PEER_HEADS * PEER_TOPK
EXPERT_ROWS = 2 * D_MODEL // 128
NORM_EPS = 1e-6
NEG_INF = -1e30

COL_LAT = 0
COL_GQ = 512
COL_MV = 2560
COL_MQ = 3072
COL_GATE = 4096
IN_COLS_PAD = 7168

VMEM_LIMIT = 56 * 1024 * 1024


def _cp(sem, vmem=None):
    return pltpu.CompilerParams(dimension_semantics=sem, vmem_limit_bytes=vmem or VMEM_LIMIT)


def _nt(a, b, precision=None):
    return lax.dot_general(a, b, (((1,), (1,)), ((), ())), precision=precision, preferred_element_type=F32)


def _tn(a, b, precision=None):
    return lax.dot_general(a, b, (((0,), (0,)), ((), ())), precision=precision, preferred_element_type=F32)


def _mm(a, b, precision=None):
    return jnp.dot(a, b, precision=precision, preferred_element_type=F32)


def _silu(x):
    return x * jax.nn.sigmoid(x)


def _mod_kernel(c_ref, w_ref, b_ref, o_ref):
    c = c_ref[...]
    o_ref[0] = _mm(_silu(c), w_ref[0], HI) + b_ref[0]


def _mod_all(c, w_mod, b_mod):
    depth, d, n = w_mod.shape
    bsz = c.shape[0]
    tn = 1024
    return pl.pallas_call(
        _mod_kernel,
        grid=(depth, n // tn),
        in_specs=[pl.BlockSpec((bsz, d), lambda l, j: (0, 0)),
                  pl.BlockSpec((1, d, tn), lambda l, j: (l, 0, j)),
                  pl.BlockSpec((1, 1, tn), lambda l, j: (l, 0, j))],
        out_specs=pl.BlockSpec((1, bsz, tn), lambda l, j: (l, 0, j)),
        out_shape=jax.ShapeDtypeStruct((depth, bsz, n), F32),
        compiler_params=_cp(("arbitrary", "arbitrary")),
        name="adaln_mod",
    )(c, w_mod, b_mod.reshape(depth, 1, n))


def _modmm_kernel(x_ref, sh_ref, sc_ref, w_ref, wlo_ref, o_ref, h_ref, hb_scr, hlo_scr, *, lo_first, lo_tiles):
    j = pl.program_id(1)

    @pl.when(j == 0)
    def _():
        x = x_ref[...]
        h = x * lax.rsqrt(jnp.mean(x * x, axis=-1, keepdims=True) + NORM_EPS)
        h = h * (1.0 + sc_ref[0]) + sh_ref[0]
        h_ref[...] = h
        hb = h.astype(BF16)
        hb_scr[...] = hb
        hlo_scr[...] = (h - hb.astype(F32)).astype(BF16)

    precise = (j >= lo_first) & (j < lo_first + lo_tiles)

    @pl.when(precise)
    def _():
        o_ref[...] = (_mm(hb_scr[...], w_ref[...]) + _mm(hb_scr[...], wlo_ref[...])
                      + _mm(hlo_scr[...], w_ref[...]))

    @pl.when(jnp.logical_not(precise))
    def _():
        o_ref[...] = _mm(hb_scr[...], w_ref[...])


def _modulate_matmul(x, shift, scale, w_bf16, w_lo, lo_first, seq, tm=512, tn=1024):
    t, d = x.shape
    n = w_bf16.shape[1]
    lo_tiles = w_lo.shape[1] // tn
    bsz = shift.shape[0]
    bidx = lambda i, j: ((i * tm) // seq, 0, 0)
    return pl.pallas_call(
        functools.partial(_modmm_kernel, lo_first=lo_first, lo_tiles=lo_tiles),
        grid=(t // tm, n // tn),
        in_specs=[pl.BlockSpec((tm, d), lambda i, j: (i, 0)),
                  pl.BlockSpec((1, 1, d), bidx),
                  pl.BlockSpec((1, 1, d), bidx),
                  pl.BlockSpec((d, tn), lambda i, j: (0, j)),
                  pl.BlockSpec((d, tn), lambda i, j: (0, jnp.clip(j - lo_first, 0, lo_tiles - 1)))],
        out_specs=[pl.BlockSpec((tm, tn), lambda i, j: (i, j)),
                   pl.BlockSpec((tm, d), lambda i, j: (i, 0))],
        out_shape=[jax.ShapeDtypeStruct((t, n), F32), jax.ShapeDtypeStruct((t, d), F32)],
        scratch_shapes=[pltpu.VMEM((tm, d), BF16), pltpu.VMEM((tm, d), BF16)],
        compiler_params=_cp(("arbitrary", "arbitrary")),
        name="modulate_matmul",
    )(x, shift.reshape(bsz, 1, d), scale.reshape(bsz, 1, d), w_bf16, w_lo)


def _mla_prep_kernel(p_ref, pos_ref, qg_ref, wuq_ref, kvg_ref, wuk_ref, wuv_ref, qng_ref, kng_ref, invf_ref,
                     q_out, k_out, v_out):
    tm = p_ref.shape[0]
    cq = p_ref[:, 0:MLA_Q_RANK]
    ckv = p_ref[:, MLA_Q_RANK:MLA_Q_RANK + MLA_KV_RANK]
    misc = p_ref[:, MLA_Q_RANK + MLA_KV_RANK:MLA_Q_RANK + MLA_KV_RANK + 128]

    def rms(v, n):
        return v * lax.rsqrt(jnp.sum(v * v, axis=-1, keepdims=True) * (1.0 / n) + NORM_EPS)

    qn = (rms(cq, MLA_Q_RANK) * qg_ref[...]).astype(BF16)
    q_all = _mm(qn, wuq_ref[...])
    kvn = (rms(ckv, MLA_KV_RANK) * kvg_ref[...]).astype(BF16)
    k_all = _mm(kvn, wuk_ref[...])
    v_out[0, 0] = _mm(kvn, wuv_ref[...]).T.astype(BF16)

    lane = lax.broadcasted_iota(jnp.int32, (tm, HEAD_PAD), 1)
    in_rope = (lane >= MLA_NOPE) & (lane < MLA_QK)
    k_rope = jnp.where(in_rope, pltpu.roll(misc, MLA_NOPE, 1), 0.0)
    ang = pos_ref[...].astype(F32) * invf_ref[...]
    cos = jnp.cos(ang)
    sin = jnp.sin(ang)
    half = MLA_ROPE // 2
    c_tab = jnp.where(lane < MLA_NOPE, 1.0, jnp.where(in_rope, cos, 0.0))
    s_lo = jnp.where(in_rope & (lane < MLA_NOPE + half), -sin, 0.0)
    s_hi = jnp.where(in_rope & (lane >= MLA_NOPE + half), sin, 0.0)

    def finish(xh, gain):
        xh = xh * lax.rsqrt(jnp.sum(xh * xh, axis=-1, keepdims=True) * (1.0 / MLA_QK) + NORM_EPS) * gain
        return xh * c_tab + pltpu.roll(xh, HEAD_PAD - half, 1) * s_lo + pltpu.roll(xh, half, 1) * s_hi

    scale = MLA_QK ** -0.5
    for h in range(MLA_HEADS):
        sl = slice(h * HEAD_PAD, (h + 1) * HEAD_PAD)
        q_out[0, 0, sl, :] = (finish(q_all[:, sl], qng_ref[...]) * scale).T.astype(BF16)
        k_out[:, sl] = finish(k_all[:, sl] + k_rope, kng_ref[...]).astype(BF16)


def _fm_spec(rows, tm, tiles_per_seq):
    return pl.BlockSpec((1, 1, rows, tm), lambda i: (i // tiles_per_seq, i % tiles_per_seq, 0, 0))


def _mla_prep(proj, pos, lw, seq):
    t = proj.shape[0]
    tm = ATTN_BLOCK
    nblk = seq // tm
    full = lambda shp: pl.BlockSpec(shp, lambda i: (0,) * len(shp))
    return pl.pallas_call(
        _mla_prep_kernel,
        grid=(t // tm,),
        in_specs=[pl.BlockSpec((tm, 512), lambda i: (i, 0)),
                  pl.BlockSpec((tm, 1), lambda i: (i, 0)),
                  full((1, MLA_Q_RANK)), full((MLA_Q_RANK, HEADS * HEAD_PAD)),
                  full((1, MLA_KV_RANK)), full((MLA_KV_RANK, HEADS * HEAD_PAD)),
                  full((MLA_KV_RANK, HEADS * MLA_V)),
                  full((1, HEAD_PAD)), full((1, HEAD_PAD)), full((1, HEAD_PAD))],
        out_specs=[_fm_spec(HEADS * HEAD_PAD, tm, nblk),
                   pl.BlockSpec((tm, HEADS * HEAD_PAD), lambda i: (i, 0)),
                   _fm_spec(HEADS * MLA_V, tm, nblk)],
        out_shape=[jax.ShapeDtypeStruct((t // seq, nblk, HEADS * HEAD_PAD, tm), BF16),
                   jax.ShapeDtypeStruct((t, HEADS * HEAD_PAD), BF16),
                   jax.ShapeDtypeStruct((t // seq, nblk, HEADS * MLA_V, tm), BF16)],
        compiler_params=_cp(("arbitrary",)),
        name="mla_prep",
    )(proj, pos, lw["mla_q_gain"], lw["mla_w_uq"], lw["mla_kv_gain"], lw["mla_w_uk"], lw["mla_w_uv"],
      lw["mla_qn_gain"], lw["mla_kn_gain"], lw["rope_inv_freq"])


def _attn_kernel(qt_ref, k_ref, vt_ref, o_ref, *, blk, hg):
    i = pl.program_id(2)
    half = blk // 2
    kidx = lax.broadcasted_iota(jnp.int32, (half, blk), 0)
    qidx = lax.broadcasted_iota(jnp.int32, (half, blk), 1)
    heads = range(hg)
    qts = [qt_ref[0, 0, hh * HEAD_PAD:(hh + 1) * HEAD_PAD, :] for hh in heads]

    chains = [(hh, kh) for hh in heads for kh in range(2)]
    n = len(chains)

    def step(j, carry, masked):
        ms, ls, accs = carry
        start = pl.multiple_of(j * blk, blk)
        rows = [pl.ds(pl.multiple_of(start + kh * half, half), half) for kh in range(2)]
        st = [_mm(k_ref[rows[kh], hh * HEAD_PAD:(hh + 1) * HEAD_PAD], qts[hh]) for hh, kh in chains]
        if masked:
            st = [jnp.where(kidx + kh * half <= qidx, s, NEG_INF) for s, (hh, kh) in zip(st, chains)]
        m_new = [jnp.maximum(ms[c], jnp.max(st[c], axis=0, keepdims=True)) for c in range(n)]
        pt = [jnp.exp(st[c] - m_new[c]) for c in range(n)]
        alpha = [jnp.exp(ms[c] - m_new[c]) for c in range(n)]
        ls = [alpha[c] * ls[c] + jnp.sum(pt[c], axis=0, keepdims=True) for c in range(n)]
        accs = [alpha[c] * accs[c]
                + _mm(vt_ref[0, j, hh * MLA_V:(hh + 1) * MLA_V, kh * half:(kh + 1) * half], pt[c].astype(BF16))
                for c, (hh, kh) in enumerate(chains)]
        return tuple(m_new), tuple(ls), tuple(accs)

    init = (tuple(jnp.full((1, blk), -jnp.inf, F32) for _ in chains),
            tuple(jnp.zeros((1, blk), F32) for _ in chains),
            tuple(jnp.zeros((MLA_V, blk), F32) for _ in chains))
    carry = lax.fori_loop(0, i, functools.partial(step, masked=False), init)
    ms, ls, accs = step(i, carry, True)
    outs = []
    for hh in heads:
        a, b = 2 * hh, 2 * hh + 1
        m = jnp.maximum(ms[a], ms[b])
        wa, wb = jnp.exp(ms[a] - m), jnp.exp(ms[b] - m)
        outs.append((wa * accs[a] + wb * accs[b]) / (wa * ls[a] + wb * ls[b]))
    o_ref[...] = jnp.concatenate(outs, axis=0).T


def _causal_attention(qt, k, vt, bsz, seq, hg=4):
    t = k.shape[0]
    blk = ATTN_BLOCK
    nq = seq // blk
    return pl.pallas_call(
        functools.partial(_attn_kernel, blk=blk, hg=hg),
        grid=(bsz, HEADS // hg, nq),
        in_specs=[pl.BlockSpec((1, 1, hg * HEAD_PAD, blk), lambda b, g, i: (b, i, g, 0)),
                  pl.BlockSpec((seq, hg * HEAD_PAD), lambda b, g, i: (b, g)),
                  pl.BlockSpec((1, nq, hg * MLA_V, blk), lambda b, g, i: (b, 0, g, 0))],
        out_specs=pl.BlockSpec((blk, hg * MLA_V), lambda b, g, i: (b * nq + i, g)),
        out_shape=jax.ShapeDtypeStruct((t, HEADS * MLA_V), F32),
        compiler_params=_cp(("arbitrary", "arbitrary", "arbitrary")),
        name="causal_attention",
    )(qt, k, vt)


def _moba_prep_kernel(mq_ref, mk_ref, mv_ref, qg_ref, kg_ref, e_ref, et_ref, q_out, k_out, v_out, kmean_scr, *, nb):
    tm = mq_ref.shape[0]
    n = pl.program_id(0) % nb

    @pl.when(n == 0)
    def _():
        kmean_scr[...] = jnp.zeros_like(kmean_scr)

    def headnorm(x, gain):
        ss = _mm(x * x, e_ref[...], HI)
        inv = lax.rsqrt(ss * (1.0 / MOBA_DH) + NORM_EPS)
        return x * _mm(inv, et_ref[...], HI) * gain

    qn = headnorm(mq_ref[...], qg_ref[...])
    kn = headnorm(mk_ref[...], kg_ref[...])
    v_out[0, 0] = mv_ref[...].T.astype(BF16)
    kmean_scr[pl.ds(n, 1), :] = jnp.mean(kn, axis=0, keepdims=True)
    km = kmean_scr[...]

    lane = lax.broadcasted_iota(jnp.int32, (tm, nb), 1)
    zpad = jnp.zeros((tm, HEAD_PAD - MOBA_DH - nb), F32)
    onehot = jnp.where(lane == n, 1.0, 0.0)
    for h in range(HEADS):
        sl = slice(h * MOBA_DH, (h + 1) * MOBA_DH)
        gate = _nt(qn[:, sl], km[:, sl], HI)
        gate = jnp.where(lane < n, gate, -jnp.inf)
        pen = jnp.full((tm, nb), NEG_INF, F32)
        for r in range(MOBA_TOPK):
            mx = jnp.max(gate, axis=-1, keepdims=True)
            idx = jnp.min(jnp.where(gate == mx, lane, nb), axis=-1, keepdims=True)
            hit = (lane == idx) & (r < n)
            pen = jnp.where(hit, 0.0, pen)
            gate = jnp.where(lane == idx, -jnp.inf, gate)
        pen = jnp.where(lane == n, 0.0, pen)
        osl = slice(h * HEAD_PAD, (h + 1) * HEAD_PAD)
        q_aug = jnp.concatenate([qn[:, sl] * (MOBA_DH ** -0.5), pen, zpad], axis=1)
        q_out[0, 0, osl, :] = q_aug.T.astype(BF16)
        k_out[:, osl] = jnp.concatenate([kn[:, sl], onehot, zpad], axis=1).astype(BF16)


def _moba_prep(proj, lw, seq):
    t = proj.shape[0]
    tm = MOBA_BLOCK
    nb = seq // tm
    full = lambda shp: pl.BlockSpec(shp, lambda i: (0,) * len(shp))
    c0 = COL_MQ // 512
    return pl.pallas_call(
        functools.partial(_moba_prep_kernel, nb=nb),
        grid=(t // tm,),
        in_specs=[pl.BlockSpec((tm, 512), lambda i: (i, c0)),
                  pl.BlockSpec((tm, 512), lambda i: (i, c0 + 1)),
                  pl.BlockSpec((tm, 512), lambda i: (i, COL_MV // 512)),
                  full((1, 512)), full((1, 512)), full((512, HEADS)), full((HEADS, 512))],
        out_specs=[_fm_spec(HEADS * HEAD_PAD, tm, nb),
                   pl.BlockSpec((tm, HEADS * HEAD_PAD), lambda i: (i, 0)),
                   _fm_spec(HEADS * MOBA_DH, tm, nb)],
        out_shape=[jax.ShapeDtypeStruct((t // seq, nb, HEADS * HEAD_PAD, tm), BF16),
                   jax.ShapeDtypeStruct((t, HEADS * HEAD_PAD), BF16),
                   jax.ShapeDtypeStruct((t // seq, nb, HEADS * MOBA_DH, tm), BF16)],
        scratch_shapes=[pltpu.VMEM((nb, HEADS * MOBA_DH), F32)],
        compiler_params=_cp(("arbitrary",)),
        name="moba_prep",
    )(proj, proj, proj, lw["moba_qn_gain"], lw["moba_kn_gain"], lw["head_sum"], lw["head_expand"])


def _gdn_prep_kernel(gq_ref, gk_ref, gv_ref, misc_ref, wq_ref, wk_ref, wv_ref, alog_ref, dtb_ref, e_ref, et_ref,
                     q_out, k_out, v_out, gb_out, ext_scr, *, tiles_per_seq):
    tm = gq_ref.shape[0]
    pad = 8

    @pl.when(pl.program_id(0) % tiles_per_seq == 0)
    def _():
        ext_scr[:, 0:pad, :] = jnp.zeros((3, pad, ext_scr.shape[2]), F32)

    def conv_silu(s, x_ref, w_ref):
        ext_scr[s, pad:pad + tm, :] = x_ref[...]
        y = jnp.zeros(x_ref.shape, F32)
        for i in range(GDN_CONV):
            y = y + ext_scr[s, pl.ds(pad - (GDN_CONV - 1) + i, tm), :] * w_ref[i:i + 1, :]
        ext_scr[s, 0:pad, :] = ext_scr[s, tm:tm + pad, :]
        return _silu(y)

    def l2n(x):
        ss = _mm(x * x, e_ref[...], HI)
        return x * _mm(lax.rsqrt(ss + NORM_EPS), et_ref[...], HI)

    q_out[...] = l2n(conv_silu(0, gq_ref, wq_ref)) * (GDN_DK ** -0.5)
    k_out[...] = l2n(conv_silu(1, gk_ref, wk_ref))
    v_out[...] = conv_silu(2, gv_ref, wv_ref)
    misc = misc_ref[...]
    a = misc[:, MLA_ROPE:MLA_ROPE + HEADS]
    b = misc[:, MLA_ROPE + HEADS:MLA_ROPE + 2 * HEADS]
    g = -jnp.exp(alog_ref[...]) * jax.nn.softplus(a + dtb_ref[...])
    beta = jax.nn.sigmoid(b)
    gb_out[...] = jnp.concatenate([g, beta, jnp.zeros((tm, 128 - 2 * HEADS), F32)], axis=1)


def _gdn_prep(proj, lw, seq, tm=256):
    t = proj.shape[0]
    full = lambda shp: pl.BlockSpec(shp, lambda i: (0,) * len(shp))
    c0 = COL_GQ // 512
    tok = lambda w: pl.BlockSpec((tm, w), lambda i: (i, 0))
    return pl.pallas_call(
        functools.partial(_gdn_prep_kernel, tiles_per_seq=seq // tm),
        grid=(t // tm,),
        in_specs=[pl.BlockSpec((tm, 512), lambda i: (i, c0)),
                  pl.BlockSpec((tm, 512), lambda i: (i, c0 + 1)),
                  pl.BlockSpec((tm, 512), lambda i: (i, c0 + 2)),
                  pl.BlockSpec((tm, 128), lambda i: (i, 3)),
                  full((GDN_CONV, 512)), full((GDN_CONV, 512)), full((GDN_CONV, 512)),
                  full((1, HEADS)), full((1, HEADS)), full((512, HEADS)), full((HEADS, 512))],
        out_specs=[tok(512), tok(512), tok(512), tok(128)],
        out_shape=[jax.ShapeDtypeStruct((t, 512), F32)] * 3 + [jax.ShapeDtypeStruct((t, 128), F32)],
        scratch_shapes=[pltpu.VMEM((3, tm + 8, 512), F32)],
        compiler_params=_cp(("arbitrary",)),
        name="gdn_prep",
    )(proj, proj, proj, proj, lw["conv_wq"], lw["conv_wk"], lw["conv_wv"], lw["gdn_a_log"], lw["gdn_dt_bias"],
      lw["head_sum"], lw["head_expand"])


def _gdn_scan_kernel(q_ref, k_ref, v_ref, gb_ref, z_ref, og_ref, o_ref, s_scr, *, chunks):
    c = GDN_CHUNK

    @pl.when(pl.program_id(1) == 0)
    def _():
        s_scr[...] = jnp.zeros_like(s_scr)

    ri = lax.broadcasted_iota(jnp.int32, (c, c), 0)
    ci = lax.broadcasted_iota(jnp.int32, (c, c), 1)
    tri = ci <= ri
    strict = ci < ri
    ltri = jnp.where(tri, 1.0, 0.0)
    probs = [(cc, h) for cc in range(chunks) for h in range(HEADS)]
    rows = lambda cc: slice(cc * c, (cc + 1) * c)
    hsl = lambda h: slice(h * GDN_DK, (h + 1) * GDN_DK)
    gbs = [gb_ref[rows(cc), :] for cc in range(chunks)]
    gcs = [_mm(ltri, gb[:, 0:HEADS], HI) for gb in gbs]
    gcts = [g.T for g in gcs]
    q = [q_ref[rows(cc), hsl(h)] for cc, h in probs]
    k = [k_ref[rows(cc), hsl(h)] for cc, h in probs]
    v = [v_ref[rows(cc), hsl(h)] for cc, h in probs]
    gcol = [gcs[cc][:, h:h + 1] for cc, h in probs]
    grow = [gcts[cc][h:h + 1, :] for cc, h in probs]
    glast = [gcs[cc][c - 1:c, h:h + 1] for cc, h in probs]
    bcol = [gbs[cc][:, HEADS + h:HEADS + h + 1] for cc, h in probs]
    n = len(probs)
    egc = [jnp.exp(g) for g in gcol]
    decay = [jnp.where(tri, jnp.exp(jnp.where(tri, gcol[i] - grow[i], 0.0)), 0.0) for i in range(n)]
    kb = [k[i] * bcol[i] for i in range(n)]
    k16 = [a.astype(BF16) for a in k]
    kb16 = [a.astype(BF16) for a in kb]
    a_neg = [jnp.where(strict, -(_nt(kb16[i], k16[i]) * decay[i]), 0.0) for i in range(n)]
    x = [jnp.concatenate([v[i] * bcol[i], kb[i] * egc[i]], axis=1) for i in range(n)]
    def mm_split_rhs(a16_, rhs):
        hi = rhs.astype(BF16)
        lo = (rhs - hi.astype(F32)).astype(BF16)
        return _mm(a16_, hi) + _mm(a16_, lo)

    for lvl in range(6):
        a16 = [a.astype(BF16) for a in a_neg]
        if lvl < 5:
            r = [mm_split_rhs(a16[i], jnp.concatenate([a_neg[i], x[i]], axis=1)) for i in range(n)]
            a_neg = [ri_[:, 0:c] for ri_ in r]
            x = [x[i] + r[i][:, c:] for i in range(n)]
        else:
            x = [x[i] + mm_split_rhs(a16[i], x[i]) for i in range(n)]
    u = [xi[:, 0:GDN_DV] for xi in x]
    w16 = [xi[:, GDN_DV:].astype(BF16) for xi in x]
    attn16 = [(_nt(q[i].astype(BF16), k16[i]) * decay[i]).astype(BF16) for i in range(n)]
    qe16 = [(q[i] * egc[i]).astype(BF16) for i in range(n)]
    kd16 = [(k[i] * jnp.exp(glast[i] - gcol[i])).astype(BF16) for i in range(n)]
    eglast = [jnp.exp(g) for g in glast]
    state = [s_scr[h] for h in range(HEADS)]
    for cc in range(chunks):
        ids = [cc * HEADS + h for h in range(HEADS)]
        s16 = [s.astype(BF16) for s in state]
        v_new = [u[i] - _mm(w16[i], s16[h]) for h, i in enumerate(ids)]
        vn16 = [a.astype(BF16) for a in v_new]
        o = [_mm(qe16[i], s16[h]) + _mm(attn16[i], vn16[h]) for h, i in enumerate(ids)]
        state = [state[h] * eglast[i] + _tn(kd16[i], vn16[h]) for h, i in enumerate(ids)]
        o = [oh * lax.rsqrt(jnp.mean(oh * oh, axis=-1, keepdims=True) + NORM_EPS) * og_ref[...] for oh in o]
        o_ref[rows(cc), :] = jnp.concatenate(o, axis=1) * _silu(z_ref[rows(cc), :])
    for h in range(HEADS):
        s_scr[h] = state[h]


def _gdn_scan(qn, kn, vv, gb, proj, lw, bsz, seq, chunks=4):
    t = qn.shape[0]
    ct = chunks * GDN_CHUNK
    ns = seq // ct
    tok = lambda w: pl.BlockSpec((ct, w), lambda b, i: (b * ns + i, 0))
    return pl.pallas_call(
        functools.partial(_gdn_scan_kernel, chunks=chunks),
        grid=(bsz, ns),
        in_specs=[tok(512), tok(512), tok(512), tok(128),
                  pl.BlockSpec((ct, 512), lambda b, i: (b * ns + i, COL_GQ // 512 + 3)),
                  pl.BlockSpec((1, GDN_DV), lambda b, i: (0, 0))],
        out_specs=tok(512),
        out_shape=jax.ShapeDtypeStruct((t, 512), F32),
        scratch_shapes=[pltpu.VMEM((HEADS, GDN_DK, GDN_DV), F32)],
        compiler_params=_cp(("arbitrary", "arbitrary")),
        name="gdn_scan",
    )(qn, kn, vv, gb, proj, lw["gdn_o_gain"])


def _merge_kernel(oa_ref, ob_ref, oc_ref, g0_ref, g1_ref, g2_ref, x_ref, ga_ref, wb_ref, wo_ref, o_ref):
    y = None
    for o_n, g_n, n in ((oa_ref, g0_ref, 0), (ob_ref, g1_ref, 1), (oc_ref, g2_ref, 2)):
        term = jax.nn.sigmoid(g_n[...]) * _mm(o_n[...].astype(BF16), wb_ref[n])
        y = term if y is None else y + term
    o_ref[...] = x_ref[...] + ga_ref[0] * _mm(y.astype(BF16), wo_ref[...])


def _merge(o_mla, o_gdn, o_moba, proj, x, g_a, lw, seq, tm=512):
    t, d = x.shape
    bsz = g_a.shape[0]
    tok = lambda w: pl.BlockSpec((tm, w), lambda i: (i, 0))
    gcol = COL_GATE // d
    return pl.pallas_call(
        _merge_kernel,
        grid=(t // tm,),
        in_specs=[tok(BRANCH_W), tok(BRANCH_W), tok(BRANCH_W),
                  pl.BlockSpec((tm, d), lambda i: (i, gcol)),
                  pl.BlockSpec((tm, d), lambda i: (i, gcol + 1)),
                  pl.BlockSpec((tm, d), lambda i: (i, gcol + 2)),
                  tok(d),
                  pl.BlockSpec((1, 1, d), lambda i: ((i * tm) // seq, 0, 0)),
                  pl.BlockSpec((N_BRANCH, BRANCH_W, d), lambda i: (0, 0, 0)),
                  pl.BlockSpec((d, d), lambda i: (0, 0))],
        out_specs=tok(d),
        out_shape=jax.ShapeDtypeStruct((t, d), F32),
        compiler_params=_cp(("arbitrary",)),
        name="branch_merge",
    )(o_mla, o_gdn, o_moba, proj, proj, proj, x, g_a.reshape(bsz, 1, d), lw["w_branch"], lw["w_out"])


def _topk_rows(s, k):
    n = s.shape[0]
    ri = lax.broadcasted_iota(jnp.int32, s.shape, 0)
    vals, idxs = [], []
    for _ in range(k):
        mx = jnp.max(s, axis=0, keepdims=True)
        idx = jnp.min(jnp.where(s == mx, ri, n), axis=0, keepdims=True)
        vals.append(mx)
        idxs.append(idx)
        s = jnp.where(ri == idx, -jnp.inf, s)
    return jnp.concatenate(vals, axis=0), jnp.concatenate(idxs, axis=0)


def _peer_select_kernel(qry_ref, keys_ref, ids_out, gates_out):
    kk = PEER_TOPK
    ids_rows, gate_rows = [], []
    for h in range(PEER_HEADS):
        halves = []
        for p in range(2):
            g = h * 2 + p
            s_t = _nt(keys_ref[g], qry_ref[:, g * 128:(g + 1) * 128], HI)
            halves.append(_topk_rows(s_t, kk))
        (v1, i1), (v2, i2) = halves
        cand, a_of, b_of = [], [], []
        for a in range(kk):
            nb = kk // (a + 1)
            cand.append(v1[a:a + 1, :] + v2[0:nb, :])
            a_of += [a] * nb
            b_of += list(range(nb))
        cand = jnp.concatenate(cand, axis=0)
        top_s, top_r = _topk_rows(cand, kk)
        a_sel = jnp.zeros_like(top_r)
        b_sel = jnp.zeros_like(top_r)
        for r, (a, b) in enumerate(zip(a_of, b_of)):
            hit = top_r == r
            a_sel = jnp.where(hit, a, a_sel)
            b_sel = jnp.where(hit, b, b_sel)
        e1 = jnp.zeros_like(top_r)
        e2 = jnp.zeros_like(top_r)
        for a in range(kk):
            e1 = jnp.where(a_sel == a, i1[a:a + 1, :], e1)
            e2 = jnp.where(b_sel == a, i2[a:a + 1, :], e2)
        ids_rows.append(e1 * PEER_NKEYS + e2)
        ex = jnp.exp(top_s - top_s[0:1, :])
        gate_rows.append(ex / jnp.sum(ex, axis=0, keepdims=True))
    ids_out[...] = jnp.concatenate(ids_rows, axis=0).T
    gates_out[...] = jnp.concatenate(gate_rows, axis=0).T


def _peer_select(qry, sub_keys, tm=256):
    t = qry.shape[0]
    return pl.pallas_call(
        _peer_select_kernel,
        grid=(t // tm,),
        in_specs=[pl.BlockSpec((tm, 2 * PEER_HEADS * 128), lambda i: (i, 0)),
                  pl.BlockSpec((2 * PEER_HEADS, PEER_NKEYS, 128), lambda i: (0, 0, 0))],
        out_specs=[pl.BlockSpec((tm, PEER_SEL), lambda i: (i, 0)),
                   pl.BlockSpec((tm, PEER_SEL), lambda i: (i, 0))],
        out_shape=[jax.ShapeDtypeStruct((t, PEER_SEL), jnp.int32), jax.ShapeDtypeStruct((t, PEER_SEL), F32)],
        compiler_params=_cp(("arbitrary",)),
        name="peer_select",
    )(qry, sub_keys)


def _peer_apply_kernel(ids_ref, gates_ref, h_ref, x_ref, gf_ref, uv_hbm, o_ref, buf0, buf1, sems, *, tt):
    s = pl.program_id(0)
    ns = pl.num_programs(0) - 1
    slot = s % 2
    buf = (buf0, buf1)

    def start_rows(dst_slot):
        for t in range(tt):
            for e in range(PEER_SEL):
                pltpu.make_async_copy(uv_hbm.at[ids_ref[t, e]], buf[dst_slot].at[t, :, e, :],
                                      sems.at[dst_slot]).start(priority=e % 2)

    def wait_rows(src_slot):
        pltpu.make_async_copy(buf[1 - src_slot], buf[src_slot], sems.at[src_slot]).wait()

    def reduce_tile(src_slot):
        d = h_ref.shape[1]
        hb = h_ref[...]
        src = buf[src_slot]
        nsub = d // 128
        acc = None
        for sub in range(nsub):
            term = src[:, sub] * hb[:, None, sub * 128:(sub + 1) * 128]
            acc = term if acc is None else acc + term
        act = jnp.sum(acc, axis=-1)
        gel = 0.5 * act * (1.0 + lax.erf(act * (2.0 ** -0.5)))
        wgt = (gates_ref[...] * gel)[:, :, None]
        outs = [jnp.sum(wgt * src[:, nsub + sub], axis=1) for sub in range(nsub)]
        o_ref[...] = x_ref[...] + gf_ref[0] * jnp.concatenate(outs, axis=-1)

    def steady(parity):
        wait_rows(1 - parity)
        start_rows(parity)
        reduce_tile(1 - parity)

    def last(parity):
        wait_rows(1 - parity)
        reduce_tile(1 - parity)

    for parity in range(2):
        pl.when((s >= 1) & (s < ns) & (slot == parity))(functools.partial(steady, parity))
    pl.when(s == 0)(functools.partial(start_rows, 0))
    for parity in range(2):
        pl.when((s == ns) & (slot == parity))(functools.partial(last, parity))


def _peer_apply(ids, gates, h2, x, g_f, uv, seq, tt=16):
    t, d = x.shape
    bsz = g_f.shape[0]
    ns = t // tt
    cur = lambda i: jnp.maximum(i - 1, 0)
    tok = lambda w: pl.BlockSpec((tt, w), lambda i: (cur(i), 0))
    return pl.pallas_call(
        functools.partial(_peer_apply_kernel, tt=tt),
        grid=(ns + 1,),
        in_specs=[pl.BlockSpec((tt, PEER_SEL), lambda i: (jnp.minimum(i, ns - 1), 0), memory_space=pltpu.SMEM),
                  tok(PEER_SEL), tok(d), tok(d),
                  pl.BlockSpec((1, 1, d), lambda i: ((cur(i) * tt) // seq, 0, 0)),
                  pl.BlockSpec(memory_space=pl.ANY)],
        out_specs=tok(d),
        out_shape=jax.ShapeDtypeStruct((t, d), F32),
        scratch_shapes=[pltpu.VMEM((tt, EXPERT_ROWS, PEER_SEL, 128), F32),
                        pltpu.VMEM((tt, EXPERT_ROWS, PEER_SEL, 128), F32),
                        pltpu.SemaphoreType.DMA((2,))],
        compiler_params=_cp(("arbitrary",)),
        name="peer_apply",
    )(ids, gates, h2, x, g_f.reshape(bsz, 1, d), uv)


def _prep_layer_weights(l, w_in, mla_q_gain, mla_w_uq, mla_kv_gain, mla_w_ukv, mla_qn_gain, mla_kn_gain,
                        gdn_conv_w, gdn_a_log, gdn_dt_bias, gdn_o_gain, moba_qn_gain, moba_kn_gain,
                        w_branch, w_out, peer_w_query, peer_sub_keys):
    d = D_MODEL
    w = w_in[l]
    o = 0
    parts = {}
    for name, width in (("cq", 256), ("ckv", 128), ("kr", 32), ("gq", 512), ("gk", 512), ("gv", 512), ("gz", 512),
                        ("ga", 8), ("gb", 8), ("mq", 512), ("mk", 512), ("mv", 512), ("gate", 3072)):
        parts[name] = w[:, o:o + width]
        o += width
    w_in_p = jnp.concatenate(
        [parts["cq"], parts["ckv"], parts["kr"], parts["ga"], parts["gb"], jnp.zeros((d, 80), F32),
         parts["gq"], parts["gk"], parts["gv"], parts["gz"], parts["mv"], parts["mq"], parts["mk"], parts["gate"]],
        axis=1)
    w_in_hi = w_in_p.astype(BF16)
    sel = slice(COL_MQ, COL_MQ + 1024)
    w_in_lo = (w_in_p[:, sel] - w_in_hi[:, sel].astype(F32)).astype(BF16)
    wq_hi = peer_w_query[l].astype(BF16)
    wq_lo = (peer_w_query[l] - wq_hi.astype(F32)).astype(BF16)

    def pad_heads(m, width):
        r = m.shape[0]
        return jnp.pad(m.reshape(r, HEADS, width), ((0, 0), (0, 0), (0, HEAD_PAD - width))).reshape(r, HEADS * HEAD_PAD)

    ukv = mla_w_ukv[l].reshape(MLA_KV_RANK, HEADS, MLA_NOPE + MLA_V)
    pad1 = lambda g, n: jnp.pad(g, (0, n - g.shape[0])).reshape(1, n)
    lane = jnp.arange(HEAD_PAD)
    half = MLA_ROPE // 2
    inv_freq = jnp.where((lane >= MLA_NOPE) & (lane < MLA_QK),
                         ROPE_THETA ** (-((lane - MLA_NOPE) % half).astype(F32) / half), 0.0).reshape(1, HEAD_PAD)
    head_sum = jnp.repeat(jnp.eye(HEADS, dtype=F32), 64, axis=0)
    return dict(
        w_in=w_in_hi, w_in_lo=w_in_lo,
        mla_q_gain=mla_q_gain[l].reshape(1, -1),
        mla_w_uq=pad_heads(mla_w_uq[l], MLA_QK).astype(BF16),
        mla_kv_gain=mla_kv_gain[l].reshape(1, -1),
        mla_w_uk=pad_heads(ukv[:, :, :MLA_NOPE].reshape(MLA_KV_RANK, -1), MLA_NOPE).astype(BF16),
        mla_w_uv=ukv[:, :, MLA_NOPE:].reshape(MLA_KV_RANK, -1).astype(BF16),
        mla_qn_gain=pad1(mla_qn_gain[l], HEAD_PAD),
        mla_kn_gain=pad1(mla_kn_gain[l], HEAD_PAD),
        rope_inv_freq=inv_freq,
        conv_wq=gdn_conv_w[l][:, 0:512], conv_wk=gdn_conv_w[l][:, 512:1024], conv_wv=gdn_conv_w[l][:, 1024:1536],
        gdn_a_log=gdn_a_log[l].reshape(1, -1), gdn_dt_bias=gdn_dt_bias[l].reshape(1, -1),
        gdn_o_gain=gdn_o_gain[l].reshape(1, -1),
        moba_qn_gain=jnp.tile(moba_qn_gain[l], HEADS).reshape(1, -1),
        moba_kn_gain=jnp.tile(moba_kn_gain[l], HEADS).reshape(1, -1),
        head_sum=head_sum, head_expand=head_sum.T,
        w_branch=w_branch[l].astype(BF16), w_out=w_out[l].astype(BF16),
        peer_w_query=wq_hi, peer_w_query_lo=wq_lo,
        peer_sub_keys=peer_sub_keys[l].reshape(2 * PEER_HEADS, PEER_NKEYS, -1),
    )


def kernel(x, c, positions, w_mod, b_mod, w_in, mla_q_gain, mla_w_uq, mla_kv_gain, mla_w_ukv, mla_qn_gain, mla_kn_gain, gdn_conv_w, gdn_a_log, gdn_dt_bias, gdn_o_gain, moba_qn_gain, moba_kn_gain, w_branch, w_out, peer_w_query, peer_sub_keys, peer_u, peer_v):
    bsz, seq, d = x.shape
    t = bsz * seq
    depth = w_mod.shape[0]
    xt = x.reshape(t, d)
    pos = positions.reshape(t, 1)
    mod = _mod_all(c, w_mod, b_mod)
    for l in range(depth):
        lw = _prep_layer_weights(l, w_in, mla_q_gain, mla_w_uq, mla_kv_gain, mla_w_ukv, mla_qn_gain, mla_kn_gain,
                                 gdn_conv_w, gdn_a_log, gdn_dt_bias, gdn_o_gain, moba_qn_gain, moba_kn_gain,
                                 w_branch, w_out, peer_w_query, peer_sub_keys)
        sh_a, sc_a, g_a, sh_f, sc_f, g_f = [mod[l, :, i * d:(i + 1) * d] for i in range(6)]
        proj, _ = _modulate_matmul(xt, sh_a, sc_a, lw["w_in"], lw["w_in_lo"], COL_MQ // 1024, seq)
        q, k, v = _mla_prep(proj, pos, lw, seq)
        o_mla = _causal_attention(q, k, v, bsz, seq)
        q, k, v = _moba_prep(proj, lw, seq)
        o_moba = _causal_attention(q, k, v, bsz, seq)
        qn, kn, vv, gb = _gdn_prep(proj, lw, seq)
        o_gdn = _gdn_scan(qn, kn, vv, gb, proj, lw, bsz, seq)
        xt = _merge(o_mla, o_gdn, o_moba, proj, xt, g_a, lw, seq)
        qry, h2 = _modulate_matmul(xt, sh_f, sc_f, lw["peer_w_query"], lw["peer_w_query_lo"], 0, seq)
        ids, gates = _peer_select(qry, lw["peer_sub_keys"])
        uv = jnp.concatenate([peer_u[l], peer_v[l]], axis=1).reshape(-1, EXPERT_ROWS, 128)
        xt = _peer_apply(ids, gates, h2, xt, g_f, uv, seq)
    return xt.reshape(bsz, seq, d)
```

```python
import functools

import jax
import jax.numpy as jnp
from jax import lax
from jax.experimental import pallas as pl
from jax.experimental.pallas import tpu as pltpu

F32 = jnp.float32
BF16 = jnp.bfloat16
HI = lax.Precision.HIGHEST

D_MODEL = 1024
MLA_HEADS = 8
MLA_Q_RANK = 256
MLA_KV_RANK = 128
MLA_NOPE = 64
MLA_ROPE = 32
MLA_V = 64
MLA_QK = MLA_NOPE + MLA_ROPE
ROPE_THETA = 10000.0
HEADS = 8
HEAD_PAD = 128
GDN_DK = 64
GDN_DV = 64
GDN_CONV = 4
GDN_CHUNK = 64
MOBA_DH = 64
MOBA_BLOCK = 256
ATTN_BLOCK = MOBA_BLOCK
MOBA_TOPK = 3
N_BRANCH = 3
BRANCH_W = 512
PEER_HEADS = 8
PEER_NKEYS = 128
PEER_TOPK = 16
PEER_SEL = PEER_HEADS * PEER_TOPK
EXPERT_ROWS = 2 * D_MODEL // 128
NORM_EPS = 1e-6
NEG_INF = -1e30

COL_LAT = 0
COL_GQ = 512
COL_MV = 2560
COL_MQ = 3072
COL_GATE = 4096
IN_COLS_PAD = 7168

VMEM_LIMIT = 56 * 1024 * 1024


def _cp(sem, vmem=None):
    return pltpu.CompilerParams(dimension_semantics=sem, vmem_limit_bytes=vmem or VMEM_LIMIT)


def _nt(a, b, precision=None):
    return lax.dot_general(a, b, (((1,), (1,)), ((), ())), precision=precision, preferred_element_type=F32)


def _tn(a, b, precision=None):
    return lax.dot_general(a, b, (((0,), (0,)), ((), ())), precision=precision, preferred_element_type=F32)


def _mm(a, b, precision=None):
    return jnp.dot(a, b, precision=precision, preferred_element_type=F32)


def _silu(x):
    return x * jax.nn.sigmoid(x)


def _mod_kernel(c_ref, w_ref, b_ref, o_ref):
    c = c_ref[...]
    o_ref[0] = _mm(_silu(c), w_ref[0], HI) + b_ref[0]


def _mod_all(c, w_mod, b_mod):
    depth, d, n = w_mod.shape
    bsz = c.shape[0]
    tn = 1024
    return pl.pallas_call(
        _mod_kernel,
        grid=(depth, n // tn),
        in_specs=[pl.BlockSpec((bsz, d), lambda l, j: (0, 0)),
                  pl.BlockSpec((1, d, tn), lambda l, j: (l, 0, j)),
                  pl.BlockSpec((1, 1, tn), lambda l, j: (l, 0, j))],
        out_specs=pl.BlockSpec((1, bsz, tn), lambda l, j: (l, 0, j)),
        out_shape=jax.ShapeDtypeStruct((depth, bsz, n), F32),
        compiler_params=_cp(("arbitrary", "arbitrary")),
        name="adaln_mod",
    )(c, w_mod, b_mod.reshape(depth, 1, n))


def _modmm_kernel(x_ref, sh_ref, sc_ref, w_ref, wlo_ref, o_ref, h_ref, hb_scr, hlo_scr, *, lo_first, lo_tiles):
    j = pl.program_id(1)

    @pl.when(j == 0)
    def _():
        x = x_ref[...]
        h = x * lax.rsqrt(jnp.mean(x * x, axis=-1, keepdims=True) + NORM_EPS)
        h = h * (1.0 + sc_ref[0]) + sh_ref[0]
        h_ref[...] = h
        hb = h.astype(BF16)
        hb_scr[...] = hb
        hlo_scr[...] = (h - hb.astype(F32)).astype(BF16)

    precise = (j >= lo_first) & (j < lo_first + lo_tiles)

    @pl.when(precise)
    def _():
        o_ref[...] = (_mm(hb_scr[...], w_ref[...]) + _mm(hb_scr[...], wlo_ref[...])
                      + _mm(hlo_scr[...], w_ref[...]))

    @pl.when(jnp.logical_not(precise))
    def _():
        o_ref[...] = _mm(hb_scr[...], w_ref[...])


def _modulate_matmul(x, shift, scale, w_bf16, w_lo, lo_first, seq, tm=1024, tn=1024):
    t, d = x.shape
    n = w_bf16.shape[1]
    lo_tiles = w_lo.shape[1] // tn
    bsz = shift.shape[0]
    bidx = lambda i, j: ((i * tm) // seq, 0, 0)
    return pl.pallas_call(
        functools.partial(_modmm_kernel, lo_first=lo_first, lo_tiles=lo_tiles),
        grid=(t // tm, n // tn),
        in_specs=[pl.BlockSpec((tm, d), lambda i, j: (i, 0)),
                  pl.BlockSpec((1, 1, d), bidx),
                  pl.BlockSpec((1, 1, d), bidx),
                  pl.BlockSpec((d, tn), lambda i, j: (0, j)),
                  pl.BlockSpec((d, tn), lambda i, j: (0, jnp.clip(j - lo_first, 0, lo_tiles - 1)))],
        out_specs=[pl.BlockSpec((tm, tn), lambda i, j: (i, j)),
                   pl.BlockSpec((tm, d), lambda i, j: (i, 0))],
        out_shape=[jax.ShapeDtypeStruct((t, n), F32), jax.ShapeDtypeStruct((t, d), F32)],
        scratch_shapes=[pltpu.VMEM((tm, d), BF16), pltpu.VMEM((tm, d), BF16)],
        compiler_params=_cp(("arbitrary", "arbitrary")),
        name="modulate_matmul",
    )(x, shift.reshape(bsz, 1, d), scale.reshape(bsz, 1, d), w_bf16, w_lo)


def _mla_prep_kernel(p_ref, pos_ref, qg_ref, wuq_ref, kvg_ref, wuk_ref, wuv_ref, qng_ref, kng_ref, invf_ref,
                     q_out, k_out, v_out):
    tm = p_ref.shape[0]
    cq = p_ref[:, 0:MLA_Q_RANK]
    ckv = p_ref[:, MLA_Q_RANK:MLA_Q_RANK + MLA_KV_RANK]
    misc = p_ref[:, MLA_Q_RANK + MLA_KV_RANK:MLA_Q_RANK + MLA_KV_RANK + 128]

    def rms(v, n):
        return v * lax.rsqrt(jnp.sum(v * v, axis=-1, keepdims=True) * (1.0 / n) + NORM_EPS)

    qn = (rms(cq, MLA_Q_RANK) * qg_ref[...]).astype(BF16)
    q_all = _mm(qn, wuq_ref[...])
    kvn = (rms(ckv, MLA_KV_RANK) * kvg_ref[...]).astype(BF16)
    k_all = _mm(kvn, wuk_ref[...])
    v_out[0, 0] = _mm(kvn, wuv_ref[...]).T.astype(BF16)

    lane = lax.broadcasted_iota(jnp.int32, (tm, HEAD_PAD), 1)
    in_rope = (lane >= MLA_NOPE) & (lane < MLA_QK)
    k_rope = jnp.where(in_rope, pltpu.roll(misc, MLA_NOPE, 1), 0.0)
    ang = pos_ref[...].astype(F32) * invf_ref[...]
    cos = jnp.cos(ang)
    sin = jnp.sin(ang)
    half = MLA_ROPE // 2
    c_tab = jnp.where(lane < MLA_NOPE, 1.0, jnp.where(in_rope, cos, 0.0))
    s_lo = jnp.where(in_rope & (lane < MLA_NOPE + half), -sin, 0.0)
    s_hi = jnp.where(in_rope & (lane >= MLA_NOPE + half), sin, 0.0)

    def finish(xh, gain):
        xh = xh * lax.rsqrt(jnp.sum(xh * xh, axis=-1, keepdims=True) * (1.0 / MLA_QK) + NORM_EPS) * gain
        return xh * c_tab + pltpu.roll(xh, HEAD_PAD - half, 1) * s_lo + pltpu.roll(xh, half, 1) * s_hi

    scale = MLA_QK ** -0.5
    for h in range(MLA_HEADS):
        sl = slice(h * HEAD_PAD, (h + 1) * HEAD_PAD)
        q_out[0, 0, sl, :] = (finish(q_all[:, sl], qng_ref[...]) * scale).T.astype(BF16)
        k_out[:, sl] = finish(k_all[:, sl] + k_rope, kng_ref[...]).astype(BF16)


def _fm_spec(rows, tm, tiles_per_seq):
    return pl.BlockSpec((1, 1, rows, tm), lambda i: (i // tiles_per_seq, i % tiles_per_seq, 0, 0))


def _mla_prep(proj, pos, lw, seq):
    t = proj.shape[0]
    tm = ATTN_BLOCK
    nblk = seq // tm
    full = lambda shp: pl.BlockSpec(shp, lambda i: (0,) * len(shp))
    return pl.pallas_call(
        _mla_prep_kernel,
        grid=(t // tm,),
        in_specs=[pl.BlockSpec((tm, 512), lambda i: (i, 0)),
                  pl.BlockSpec((tm, 1), lambda i: (i, 0)),
                  full((1, MLA_Q_RANK)), full((MLA_Q_RANK, HEADS * HEAD_PAD)),
                  full((1, MLA_KV_RANK)), full((MLA_KV_RANK, HEADS * HEAD_PAD)),
                  full((MLA_KV_RANK, HEADS * MLA_V)),
                  full((1, HEAD_PAD)), full((1, HEAD_PAD)), full((1, HEAD_PAD))],
        out_specs=[_fm_spec(HEADS * HEAD_PAD, tm, nblk),
                   pl.BlockSpec((tm, HEADS * HEAD_PAD), lambda i: (i, 0)),
                   _fm_spec(HEADS * MLA_V, tm, nblk)],
        out_shape=[jax.ShapeDtypeStruct((t // seq, nblk, HEADS * HEAD_PAD, tm), BF16),
                   jax.ShapeDtypeStruct((t, HEADS * HEAD_PAD), BF16),
                   jax.ShapeDtypeStruct((t // seq, nblk, HEADS * MLA_V, tm), BF16)],
        compiler_params=_cp(("arbitrary",)),
        name="mla_prep",
    )(proj, pos, lw["mla_q_gain"], lw["mla_w_uq"], lw["mla_kv_gain"], lw["mla_w_uk"], lw["mla_w_uv"],
      lw["mla_qn_gain"], lw["mla_kn_gain"], lw["rope_inv_freq"])


def _attn_kernel(qt_ref, k_ref, vt_ref, o_ref, *, blk, hg):
    i = pl.program_id(2)
    half = blk // 2
    kidx = lax.broadcasted_iota(jnp.int32, (half, blk), 0)
    qidx = lax.broadcasted_iota(jnp.int32, (half, blk), 1)
    heads = range(hg)
    qts = [qt_ref[0, 0, hh * HEAD_PAD:(hh + 1) * HEAD_PAD, :] for hh in heads]

    chains = [(hh, kh) for hh in heads for kh in range(2)]
    n = len(chains)

    def scores(j):
        start = pl.multiple_of(j * blk, blk)
        rows = [pl.ds(pl.multiple_of(start + kh * half, half), half) for kh in range(2)]
        return [_mm(k_ref[rows[kh], hh * HEAD_PAD:(hh + 1) * HEAD_PAD], qts[hh]) for hh, kh in chains]

    def update(j, st, carry, masked):
        ms, ls, accs = carry
        if masked:
            st = [jnp.where(kidx + kh * half <= qidx, s, NEG_INF) for s, (hh, kh) in zip(st, chains)]
        m_new = [jnp.maximum(ms[c], jnp.max(st[c], axis=0, keepdims=True)) for c in range(n)]
        pt = [jnp.exp(st[c] - m_new[c]) for c in range(n)]
        alpha = [jnp.exp(ms[c] - m_new[c]) for c in range(n)]
        ls = [alpha[c] * ls[c] + jnp.sum(pt[c], axis=0, keepdims=True) for c in range(n)]
        accs = [alpha[c] * accs[c]
                + _mm(vt_ref[0, j, hh * MLA_V:(hh + 1) * MLA_V, kh * half:(kh + 1) * half], pt[c].astype(BF16))
                for c, (hh, kh) in enumerate(chains)]
        return tuple(m_new), tuple(ls), tuple(accs)

    def pair(p, carry):
        j = 2 * p
        st0 = scores(j)
        st1 = scores(j + 1)
        carry = update(j, st0, carry, False)
        return update(j + 1, st1, carry, False)

    def finish(carry):
        ms, ls, accs = carry
        outs = []
        for hh in heads:
            a, b = 2 * hh, 2 * hh + 1
            m = jnp.maximum(ms[a], ms[b])
            wa, wb = jnp.exp(ms[a] - m), jnp.exp(ms[b] - m)
            outs.append((wa * accs[a] + wb * accs[b]) / (wa * ls[a] + wb * ls[b]))
        o_ref[...] = jnp.concatenate(outs, axis=0).T

    init = (tuple(jnp.full((1, blk), -jnp.inf, F32) for _ in chains),
            tuple(jnp.zeros((1, blk), F32) for _ in chains),
            tuple(jnp.zeros((MLA_V, blk), F32) for _ in chains))
    carry = lax.fori_loop(0, i // 2, pair, init)

    @pl.when(i % 2 == 0)
    def _():
        finish(update(i, scores(i), carry, True))

    @pl.when(i % 2 == 1)
    def _():
        st0 = scores(i - 1)
        st1 = scores(i)
        finish(update(i, st1, update(i - 1, st0, carry, False), True))


def _causal_attention(qt, k, vt, bsz, seq, hg=4):
    t = k.shape[0]
    blk = ATTN_BLOCK
    nq = seq // blk
    return pl.pallas_call(
        functools.partial(_attn_kernel, blk=blk, hg=hg),
        grid=(bsz, HEADS // hg, nq),
        in_specs=[pl.BlockSpec((1, 1, hg * HEAD_PAD, blk), lambda b, g, i: (b, i, g, 0)),
                  pl.BlockSpec((seq, hg * HEAD_PAD), lambda b, g, i: (b, g)),
                  pl.BlockSpec((1, nq, hg * MLA_V, blk), lambda b, g, i: (b, 0, g, 0))],
        out_specs=pl.BlockSpec((blk, hg * MLA_V), lambda b, g, i: (b * nq + i, g)),
        out_shape=jax.ShapeDtypeStruct((t, HEADS * MLA_V), F32),
        compiler_params=_cp(("arbitrary", "arbitrary", "arbitrary")),
        name="causal_attention",
    )(qt, k, vt)


def _moba_prep_kernel(mq_ref, mk_ref, mv_ref, qg_ref, kg_ref, e_ref, et_ref, q_out, k_out, v_out, kmean_scr, *, nb):
    tm = mq_ref.shape[0]
    n = pl.program_id(0) % nb

    @pl.when(n == 0)
    def _():
        kmean_scr[...] = jnp.zeros_like(kmean_scr)

    def headnorm(x, gain):
        ss = _mm(x * x, e_ref[...], HI)
        inv = lax.rsqrt(ss * (1.0 / MOBA_DH) + NORM_EPS)
        return x * _mm(inv, et_ref[...], HI) * gain

    qn = headnorm(mq_ref[...], qg_ref[...])
    kn = headnorm(mk_ref[...], kg_ref[...])
    v_out[0, 0] = mv_ref[...].T.astype(BF16)
    kmean_scr[pl.ds(n, 1), :] = jnp.mean(kn, axis=0, keepdims=True)
    km = kmean_scr[...]

    lane = lax.broadcasted_iota(jnp.int32, (tm, nb), 1)
    zpad = jnp.zeros((tm, HEAD_PAD - MOBA_DH - nb), F32)
    onehot = jnp.where(lane == n, 1.0, 0.0)
    for h in range(HEADS):
        sl = slice(h * MOBA_DH, (h + 1) * MOBA_DH)
        gate = _nt(qn[:, sl], km[:, sl], HI)
        gate = jnp.where(lane < n, gate, -jnp.inf)
        pen = jnp.full((tm, nb), NEG_INF, F32)
        for r in range(MOBA_TOPK):
            mx = jnp.max(gate, axis=-1, keepdims=True)
            idx = jnp.min(jnp.where(gate == mx, lane, nb), axis=-1, keepdims=True)
            hit = (lane == idx) & (r < n)
            pen = jnp.where(hit, 0.0, pen)
            gate = jnp.where(lane == idx, -jnp.inf, gate)
        pen = jnp.where(lane == n, 0.0, pen)
        osl = slice(h * HEAD_PAD, (h + 1) * HEAD_PAD)
        q_aug = jnp.concatenate([qn[:, sl] * (MOBA_DH ** -0.5), pen, zpad], axis=1)
        q_out[0, 0, osl, :] = q_aug.T.astype(BF16)
        k_out[:, osl] = jnp.concatenate([kn[:, sl], onehot, zpad], axis=1).astype(BF16)


def _moba_prep(proj, lw, seq):
    t = proj.shape[0]
    tm = MOBA_BLOCK
    nb = seq // tm
    full = lambda shp: pl.BlockSpec(shp, lambda i: (0,) * len(shp))
    c0 = COL_MQ // 512
    return pl.pallas_call(
        functools.partial(_moba_prep_kernel, nb=nb),
        grid=(t // tm,),
        in_specs=[pl.BlockSpec((tm, 512), lambda i: (i, c0)),
                  pl.BlockSpec((tm, 512), lambda i: (i, c0 + 1)),
                  pl.BlockSpec((tm, 512), lambda i: (i, COL_MV // 512)),
                  full((1, 512)), full((1, 512)), full((512, HEADS)), full((HEADS, 512))],
        out_specs=[_fm_spec(HEADS * HEAD_PAD, tm, nb),
                   pl.BlockSpec((tm, HEADS * HEAD_PAD), lambda i: (i, 0)),
                   _fm_spec(HEADS * MOBA_DH, tm, nb)],
        out_shape=[jax.ShapeDtypeStruct((t // seq, nb, HEADS * HEAD_PAD, tm), BF16),
                   jax.ShapeDtypeStruct((t, HEADS * HEAD_PAD), BF16),
                   jax.ShapeDtypeStruct((t // seq, nb, HEADS * MOBA_DH, tm), BF16)],
        scratch_shapes=[pltpu.VMEM((nb, HEADS * MOBA_DH), F32)],
        compiler_params=_cp(("arbitrary",)),
        name="moba_prep",
    )(proj, proj, proj, lw["moba_qn_gain"], lw["moba_kn_gain"], lw["head_sum"], lw["head_expand"])


def _gdn_prep_kernel(gq_ref, gk_ref, gv_ref, misc_ref, wq_ref, wk_ref, wv_ref, alog_ref, dtb_ref, e_ref, et_ref,
                     q_out, k_out, v_out, gb_out, ext_scr, *, tiles_per_seq):
    tm = gq_ref.shape[0]
    pad = 8

    @pl.when(pl.program_id(0) % tiles_per_seq == 0)
    def _():
        ext_scr[:, 0:pad, :] = jnp.zeros((3, pad, ext_scr.shape[2]), F32)

    def conv_silu(s, x_ref, w_ref):
        ext_scr[s, pad:pad + tm, :] = x_ref[...]
        y = jnp.zeros(x_ref.shape, F32)
        for i in range(GDN_CONV):
            y = y + ext_scr[s, pl.ds(pad - (GDN_CONV - 1) + i, tm), :] * w_ref[i:i + 1, :]
        ext_scr[s, 0:pad, :] = ext_scr[s, tm:tm + pad, :]
        return _silu(y)

    def l2n(x):
        ss = _mm(x * x, e_ref[...], HI)
        return x * _mm(lax.rsqrt(ss + NORM_EPS), et_ref[...], HI)

    q_out[...] = l2n(conv_silu(0, gq_ref, wq_ref)) * (GDN_DK ** -0.5)
    k_out[...] = l2n(conv_silu(1, gk_ref, wk_ref))
    v_out[...] = conv_silu(2, gv_ref, wv_ref)
    misc = misc_ref[...]
    a = misc[:, MLA_ROPE:MLA_ROPE + HEADS]
    b = misc[:, MLA_ROPE + HEADS:MLA_ROPE + 2 * HEADS]
    g = -jnp.exp(alog_ref[...]) * jax.nn.softplus(a + dtb_ref[...])
    beta = jax.nn.sigmoid(b)
    gb_out[...] = jnp.concatenate([g, beta, jnp.zeros((tm, 128 - 2 * HEADS), F32)], axis=1)


def _gdn_prep(proj, lw, seq, tm=256):
    t = proj.shape[0]
    full = lambda shp: pl.BlockSpec(shp, lambda i: (0,) * len(shp))
    c0 = COL_GQ // 512
    tok = lambda w: pl.BlockSpec((tm, w), lambda i: (i, 0))
    return pl.pallas_call(
        functools.partial(_gdn_prep_kernel, tiles_per_seq=seq // tm),
        grid=(t // tm,),
        in_specs=[pl.BlockSpec((tm, 512), lambda i: (i, c0)),
                  pl.BlockSpec((tm, 512), lambda i: (i, c0 + 1)),
                  pl.BlockSpec((tm, 512), lambda i: (i, c0 + 2)),
                  pl.BlockSpec((tm, 128), lambda i: (i, 3)),
                  full((GDN_CONV, 512)), full((GDN_CONV, 512)), full((GDN_CONV, 512)),
                  full((1, HEADS)), full((1, HEADS)), full((512, HEADS)), full((HEADS, 512))],
        out_specs=[tok(512), tok(512), tok(512), tok(128)],
        out_shape=[jax.ShapeDtypeStruct((t, 512), F32)] * 3 + [jax.ShapeDtypeStruct((t, 128), F32)],
        scratch_shapes=[pltpu.VMEM((3, tm + 8, 512), F32)],
        compiler_params=_cp(("arbitrary",)),
        name="gdn_prep",
    )(proj, proj, proj, proj, lw["conv_wq"], lw["conv_wk"], lw["conv_wv"], lw["gdn_a_log"], lw["gdn_dt_bias"],
      lw["head_sum"], lw["head_expand"])


def _gdn_scan_kernel(q_ref, k_ref, v_ref, gb_ref, z_ref, og_ref, o_ref, s_scr, *, chunks):
    c = GDN_CHUNK

    @pl.when(pl.program_id(1) == 0)
    def _():
        s_scr[...] = jnp.zeros_like(s_scr)

    ri = lax.broadcasted_iota(jnp.int32, (c, c), 0)
    ci = lax.broadcasted_iota(jnp.int32, (c, c), 1)
    tri = ci <= ri
    strict = ci < ri
    ltri = jnp.where(tri, 1.0, 0.0)
    probs = [(cc, h) for cc in range(chunks) for h in range(HEADS)]
    rows = lambda cc: slice(cc * c, (cc + 1) * c)
    hsl = lambda h: slice(h * GDN_DK, (h + 1) * GDN_DK)
    gbs = [gb_ref[rows(cc), :] for cc in range(chunks)]
    gcs = [_mm(ltri, gb[:, 0:HEADS], HI) for gb in gbs]
    gcts = [g.T for g in gcs]
    q = [q_ref[rows(cc), hsl(h)] for cc, h in probs]
    k = [k_ref[rows(cc), hsl(h)] for cc, h in probs]
    v = [v_ref[rows(cc), hsl(h)] for cc, h in probs]
    gcol = [gcs[cc][:, h:h + 1] for cc, h in probs]
    grow = [gcts[cc][h:h + 1, :] for cc, h in probs]
    glast = [gcs[cc][c - 1:c, h:h + 1] for cc, h in probs]
    bcol = [gbs[cc][:, HEADS + h:HEADS + h + 1] for cc, h in probs]
    n = len(probs)
    egc = [jnp.exp(g) for g in gcol]
    decay = [jnp.where(tri, jnp.exp(jnp.where(tri, gcol[i] - grow[i], 0.0)), 0.0) for i in range(n)]
    kb = [k[i] * bcol[i] for i in range(n)]
    k16 = [a.astype(BF16) for a in k]
    kb16 = [a.astype(BF16) for a in kb]
    a_neg = [jnp.where(strict, -(_nt(kb16[i], k16[i]) * decay[i]), 0.0) for i in range(n)]
    x = [jnp.concatenate([v[i] * bcol[i], kb[i] * egc[i]], axis=1) for i in range(n)]
    def mm_split_rhs(a16_, rhs):
        hi = rhs.astype(BF16)
        lo = (rhs - hi.astype(F32)).astype(BF16)
        return _mm(a16_, hi) + _mm(a16_, lo)

    for lvl in range(6):
        a16 = [a.astype(BF16) for a in a_neg]
        if lvl < 5:
            r = [mm_split_rhs(a16[i], jnp.concatenate([a_neg[i], x[i]], axis=1)) for i in range(n)]
            a_neg = [ri_[:, 0:c] for ri_ in r]
            x = [x[i] + r[i][:, c:] for i in range(n)]
        else:
            x = [x[i] + mm_split_rhs(a16[i], x[i]) for i in range(n)]
    u = [xi[:, 0:GDN_DV] for xi in x]
    w16 = [xi[:, GDN_DV:].astype(BF16) for xi in x]
    attn16 = [(_nt(q[i].astype(BF16), k16[i]) * decay[i]).astype(BF16) for i in range(n)]
    qe16 = [(q[i] * egc[i]).astype(BF16) for i in range(n)]
    kd16 = [(k[i] * jnp.exp(glast[i] - gcol[i])).astype(BF16) for i in range(n)]
    eglast = [jnp.exp(g) for g in glast]
    state = [s_scr[h] for h in range(HEADS)]
    for cc in range(chunks):
        ids = [cc * HEADS + h for h in range(HEADS)]
        s16 = [s.astype(BF16) for s in state]
        v_new = [u[i] - _mm(w16[i], s16[h]) for h, i in enumerate(ids)]
        vn16 = [a.astype(BF16) for a in v_new]
        o = [_mm(qe16[i], s16[h]) + _mm(attn16[i], vn16[h]) for h, i in enumerate(ids)]
        state = [state[h] * eglast[i] + _tn(kd16[i], vn16[h]) for h, i in enumerate(ids)]
        o = [oh * lax.rsqrt(jnp.mean(oh * oh, axis=-1, keepdims=True) + NORM_EPS) * og_ref[...] for oh in o]
        o_ref[rows(cc), :] = jnp.concatenate(o, axis=1) * _silu(z_ref[rows(cc), :])
    for h in range(HEADS):
        s_scr[h] = state[h]


def _gdn_scan(qn, kn, vv, gb, proj, lw, bsz, seq, chunks=4):
    t = qn.shape[0]
    ct = chunks * GDN_CHUNK
    ns = seq // ct
    tok = lambda w: pl.BlockSpec((ct, w), lambda b, i: (b * ns + i, 0))
    return pl.pallas_call(
        functools.partial(_gdn_scan_kernel, chunks=chunks),
        grid=(bsz, ns),
        in_specs=[tok(512), tok(512), tok(512), tok(128),
                  pl.BlockSpec((ct, 512), lambda b, i: (b * ns + i, COL_GQ // 512 + 3)),
                  pl.BlockSpec((1, GDN_DV), lambda b, i: (0, 0))],
        out_specs=tok(512),
        out_shape=jax.ShapeDtypeStruct((t, 512), F32),
        scratch_shapes=[pltpu.VMEM((HEADS, GDN_DK, GDN_DV), F32)],
        compiler_params=_cp(("arbitrary", "arbitrary")),
        name="gdn_scan",
    )(qn, kn, vv, gb, proj, lw["gdn_o_gain"])


def _merge_kernel(oa_ref, ob_ref, oc_ref, g0_ref, g1_ref, g2_ref, x_ref, ga_ref, wb_ref, wo_ref, o_ref):
    y = None
    for o_n, g_n, n in ((oa_ref, g0_ref, 0), (ob_ref, g1_ref, 1), (oc_ref, g2_ref, 2)):
        term = jax.nn.sigmoid(g_n[...]) * _mm(o_n[...].astype(BF16), wb_ref[n])
        y = term if y is None else y + term
    o_ref[...] = x_ref[...] + ga_ref[0] * _mm(y.astype(BF16), wo_ref[...])


def _merge(o_mla, o_gdn, o_moba, proj, x, g_a, lw, seq, tm=512):
    t, d = x.shape
    bsz = g_a.shape[0]
    tok = lambda w: pl.BlockSpec((tm, w), lambda i: (i, 0))
    gcol = COL_GATE // d
    return pl.pallas_call(
        _merge_kernel,
        grid=(t // tm,),
        in_specs=[tok(BRANCH_W), tok(BRANCH_W), tok(BRANCH_W),
                  pl.BlockSpec((tm, d), lambda i: (i, gcol)),
                  pl.BlockSpec((tm, d), lambda i: (i, gcol + 1)),
                  pl.BlockSpec((tm, d), lambda i: (i, gcol + 2)),
                  tok(d),
                  pl.BlockSpec((1, 1, d), lambda i: ((i * tm) // seq, 0, 0)),
                  pl.BlockSpec((N_BRANCH, BRANCH_W, d), lambda i: (0, 0, 0)),
                  pl.BlockSpec((d, d), lambda i: (0, 0))],
        out_specs=tok(d),
        out_shape=jax.ShapeDtypeStruct((t, d), F32),
        compiler_params=_cp(("arbitrary",)),
        name="branch_merge",
    )(o_mla, o_gdn, o_moba, proj, proj, proj, x, g_a.reshape(bsz, 1, d), lw["w_branch"], lw["w_out"])


def _topk_rows(s, k):
    n = s.shape[0]
    ri = lax.broadcasted_iota(jnp.int32, s.shape, 0).astype(F32)
    vals, idxs = [], []
    for _ in range(k):
        mx = jnp.max(s, axis=0, keepdims=True)
        idx = jnp.min(jnp.where(s == mx, ri, float(n)), axis=0, keepdims=True)
        vals.append(mx)
        idxs.append(idx)
        s = jnp.where(ri == idx, -jnp.inf, s)
    return jnp.concatenate(vals, axis=0), jnp.concatenate(idxs, axis=0).astype(jnp.int32)


def _peer_select_kernel(qry_ref, keys_ref, ids_out, gates_out):
    kk = PEER_TOPK
    ids_rows, gate_rows = [], []
    for h in range(PEER_HEADS):
        halves = []
        for p in range(2):
            g = h * 2 + p
            s_t = _nt(keys_ref[g], qry_ref[:, g * 128:(g + 1) * 128], HI)
            halves.append(_topk_rows(s_t, kk))
        (v1, i1), (v2, i2) = halves
        cand, a_of, b_of = [], [], []
        for a in range(kk):
            nb = kk // (a + 1)
            cand.append(v1[a:a + 1, :] + v2[0:nb, :])
            a_of += [a] * nb
            b_of += list(range(nb))
        cand = jnp.concatenate(cand, axis=0)
        top_s, top_r = _topk_rows(cand, kk)
        a_sel = jnp.zeros_like(top_r)
        b_sel = jnp.zeros_like(top_r)
        for r, (a, b) in enumerate(zip(a_of, b_of)):
            hit = top_r == r
            a_sel = jnp.where(hit, a, a_sel)
            b_sel = jnp.where(hit, b, b_sel)
        e1 = jnp.zeros_like(top_r)
        e2 = jnp.zeros_like(top_r)
        for a in range(kk):
            e1 = jnp.where(a_sel == a, i1[a:a + 1, :], e1)
            e2 = jnp.where(b_sel == a, i2[a:a + 1, :], e2)
        ids_rows.append(e1 * PEER_NKEYS + e2)
        ex = jnp.exp(top_s - top_s[0:1, :])
        gate_rows.append(ex / jnp.sum(ex, axis=0, keepdims=True))
    ids_out[...] = jnp.concatenate(ids_rows, axis=0).T
    gates_out[...] = jnp.concatenate(gate_rows, axis=0).T


def _peer_select(qry, sub_keys, tm=256):
    t = qry.shape[0]
    return pl.pallas_call(
        _peer_select_kernel,
        grid=(t // tm,),
        in_specs=[pl.BlockSpec((tm, 2 * PEER_HEADS * 128), lambda i: (i, 0)),
                  pl.BlockSpec((2 * PEER_HEADS, PEER_NKEYS, 128), lambda i: (0, 0, 0))],
        out_specs=[pl.BlockSpec((tm, PEER_SEL), lambda i: (i, 0)),
                   pl.BlockSpec((tm, PEER_SEL), lambda i: (i, 0))],
        out_shape=[jax.ShapeDtypeStruct((t, PEER_SEL), jnp.int32), jax.ShapeDtypeStruct((t, PEER_SEL), F32)],
        compiler_params=_cp(("arbitrary",)),
        name="peer_select",
    )(qry, sub_keys)


def _peer_apply_kernel(ids_ref, gates_ref, h_ref, x_ref, gf_ref, uv_hbm, o_ref, buf0, buf1, sems, *, tt):
    s = pl.program_id(0)
    ns = pl.num_programs(0) - 1
    slot = s % 2
    buf = (buf0, buf1)

    def start_rows(dst_slot):
        for t in range(tt):
            for e in range(PEER_SEL):
                pltpu.make_async_copy(uv_hbm.at[ids_ref[t, e]], buf[dst_slot].at[t, :, e, :],
                                      sems.at[dst_slot]).start(priority=e % 2)

    def wait_rows(src_slot):
        pltpu.make_async_copy(buf[1 - src_slot], buf[src_slot], sems.at[src_slot]).wait()

    def reduce_tile(src_slot):
        d = h_ref.shape[1]
        hb = h_ref[...]
        src = buf[src_slot]
        nsub = d // 128
        acc = None
        for sub in range(nsub):
            term = src[:, sub] * hb[:, None, sub * 128:(sub + 1) * 128]
            acc = term if acc is None else acc + term
        act = jnp.sum(acc, axis=-1)
        gel = 0.5 * act * (1.0 + lax.erf(act * (2.0 ** -0.5)))
        wgt = (gates_ref[...] * gel)[:, :, None]
        outs = [jnp.sum(wgt * src[:, nsub + sub], axis=1) for sub in range(nsub)]
        o_ref[...] = x_ref[...] + gf_ref[0] * jnp.concatenate(outs, axis=-1)

    def steady(parity):
        wait_rows(1 - parity)
        start_rows(parity)
        reduce_tile(1 - parity)

    def last(parity):
        wait_rows(1 - parity)
        reduce_tile(1 - parity)

    for parity in range(2):
        pl.when((s >= 1) & (s < ns) & (slot == parity))(functools.partial(steady, parity))
    pl.when(s == 0)(functools.partial(start_rows, 0))
    for parity in range(2):
        pl.when((s == ns) & (slot == parity))(functools.partial(last, parity))


def _peer_apply(ids, gates, h2, x, g_f, uv, seq, tt=16):
    t, d = x.shape
    bsz = g_f.shape[0]
    ns = t // tt
    cur = lambda i: jnp.maximum(i - 1, 0)
    tok = lambda w: pl.BlockSpec((tt, w), lambda i: (cur(i), 0))
    return pl.pallas_call(
        functools.partial(_peer_apply_kernel, tt=tt),
        grid=(ns + 1,),
        in_specs=[pl.BlockSpec((tt, PEER_SEL), lambda i: (jnp.minimum(i, ns - 1), 0), memory_space=pltpu.SMEM),
                  tok(PEER_SEL), tok(d), tok(d),
                  pl.BlockSpec((1, 1, d), lambda i: ((cur(i) * tt) // seq, 0, 0)),
                  pl.BlockSpec(memory_space=pl.ANY)],
        out_specs=tok(d),
        out_shape=jax.ShapeDtypeStruct((t, d), F32),
        scratch_shapes=[pltpu.VMEM((tt, EXPERT_ROWS, PEER_SEL, 128), F32),
                        pltpu.VMEM((tt, EXPERT_ROWS, PEER_SEL, 128), F32),
                        pltpu.SemaphoreType.DMA((2,))],
        compiler_params=_cp(("arbitrary",)),
        name="peer_apply",
    )(ids, gates, h2, x, g_f.reshape(bsz, 1, d), uv)


def _prep_layer_weights(l, w_in, mla_q_gain, mla_w_uq, mla_kv_gain, mla_w_ukv, mla_qn_gain, mla_kn_gain,
                        gdn_conv_w, gdn_a_log, gdn_dt_bias, gdn_o_gain, moba_qn_gain, moba_kn_gain,
                        w_branch, w_out, peer_w_query, peer_sub_keys):
    d = D_MODEL
    w = w_in[l]
    o = 0
    parts = {}
    for name, width in (("cq", 256), ("ckv", 128), ("kr", 32), ("gq", 512), ("gk", 512), ("gv", 512), ("gz", 512),
                        ("ga", 8), ("gb", 8), ("mq", 512), ("mk", 512), ("mv", 512), ("gate", 3072)):
        parts[name] = w[:, o:o + width]
        o += width
    w_in_p = jnp.concatenate(
        [parts["cq"], parts["ckv"], parts["kr"], parts["ga"], parts["gb"], jnp.zeros((d, 80), F32),
         parts["gq"], parts["gk"], parts["gv"], parts["gz"], parts["mv"], parts["mq"], parts["mk"], parts["gate"]],
        axis=1)
    w_in_hi = w_in_p.astype(BF16)
    sel = slice(COL_MQ, COL_MQ + 1024)
    w_in_lo = (w_in_p[:, sel] - w_in_hi[:, sel].astype(F32)).astype(BF16)
    wq_hi = peer_w_query[l].astype(BF16)
    wq_lo = (peer_w_query[l] - wq_hi.astype(F32)).astype(BF16)

    def pad_heads(m, width):
        r = m.shape[0]
        return jnp.pad(m.reshape(r, HEADS, width), ((0, 0), (0, 0), (0, HEAD_PAD - width))).reshape(r, HEADS * HEAD_PAD)

    ukv = mla_w_ukv[l].reshape(MLA_KV_RANK, HEADS, MLA_NOPE + MLA_V)
    pad1 = lambda g, n: jnp.pad(g, (0, n - g.shape[0])).reshape(1, n)
    lane = jnp.arange(HEAD_PAD)
    half = MLA_ROPE // 2
    inv_freq = jnp.where((lane >= MLA_NOPE) & (lane < MLA_QK),
                         ROPE_THETA ** (-((lane - MLA_NOPE) % half).astype(F32) / half), 0.0).reshape(1, HEAD_PAD)
    head_sum = jnp.repeat(jnp.eye(HEADS, dtype=F32), 64, axis=0)
    return dict(
        w_in=w_in_hi, w_in_lo=w_in_lo,
        mla_q_gain=mla_q_gain[l].reshape(1, -1),
        mla_w_uq=pad_heads(mla_w_uq[l], MLA_QK).astype(BF16),
        mla_kv_gain=mla_kv_gain[l].reshape(1, -1),
        mla_w_uk=pad_heads(ukv[:, :, :MLA_NOPE].reshape(MLA_KV_RANK, -1), MLA_NOPE).astype(BF16),
        mla_w_uv=ukv[:, :, MLA_NOPE:].reshape(MLA_KV_RANK, -1).astype(BF16),
        mla_qn_gain=pad1(mla_qn_gain[l], HEAD_PAD),
        mla_kn_gain=pad1(mla_kn_gain[l], HEAD_PAD),
        rope_inv_freq=inv_freq,
        conv_wq=gdn_conv_w[l][:, 0:512], conv_wk=gdn_conv_w[l][:, 512:1024], conv_wv=gdn_conv_w[l][:, 1024:1536],
        gdn_a_log=gdn_a_log[l].reshape(1, -1), gdn_dt_bias=gdn_dt_bias[l].reshape(1, -1),
        gdn_o_gain=gdn_o_gain[l].reshape(1, -1),
        moba_qn_gain=jnp.tile(moba_qn_gain[l], HEADS).reshape(1, -1),
        moba_kn_gain=jnp.tile(moba_kn_gain[l], HEADS).reshape(1, -1),
        head_sum=head_sum, head_expand=head_sum.T,
        w_branch=w_branch[l].astype(BF16), w_out=w_out[l].astype(BF16),
        peer_w_query=wq_hi, peer_w_query_lo=wq_lo,
        peer_sub_keys=peer_sub_keys[l].reshape(2 * PEER_HEADS, PEER_NKEYS, -1),
    )


def kernel(x, c, positions, w_mod, b_mod, w_in, mla_q_gain, mla_w_uq, mla_kv_gain, mla_w_ukv, mla_qn_gain, mla_kn_gain, gdn_conv_w, gdn_a_log, gdn_dt_bias, gdn_o_gain, moba_qn_gain, moba_kn_gain, w_branch, w_out, peer_w_query, peer_sub_keys, peer_u, peer_v):
    bsz, seq, d = x.shape
    t = bsz * seq
    depth = w_mod.shape[0]
    xt = x.reshape(t, d)
    pos = positions.reshape(t, 1)
    mod = _mod_all(c, w_mod, b_mod)
    for l in range(depth):
        lw = _prep_layer_weights(l, w_in, mla_q_gain, mla_w_uq, mla_kv_gain, mla_w_ukv, mla_qn_gain, mla_kn_gain,
                                 gdn_conv_w, gdn_a_log, gdn_dt_bias, gdn_o_gain, moba_qn_gain, moba_kn_gain,
                                 w_branch, w_out, peer_w_query, peer_sub_keys)
        sh_a, sc_a, g_a, sh_f, sc_f, g_f = [mod[l, :, i * d:(i + 1) * d] for i in range(6)]
        proj, _ = _modulate_matmul(xt, sh_a, sc_a, lw["w_in"], lw["w_in_lo"], COL_MQ // 1024, seq)
        q, k, v = _mla_prep(proj, pos, lw, seq)
        o_mla = _causal_attention(q, k, v, bsz, seq)
        q, k, v = _moba_prep(proj, lw, seq)
        o_moba = _causal_attention(q, k, v, bsz, seq)
        qn, kn, vv, gb = _gdn_prep(proj, lw, seq)
        o_gdn = _gdn_scan(qn, kn, vv, gb, proj, lw, bsz, seq)
        xt = _merge(o_mla, o_gdn, o_moba, proj, xt, g_a, lw, seq)
        qry, h2 = _modulate_matmul(xt, sh_f, sc_f, lw["peer_w_query"], lw["peer_w_query_lo"], 0, seq)
        ids, gates = _peer_select(qry, lw["peer_sub_keys"])
        uv = jnp.concatenate([peer_u[l], peer_v[l]], axis=1).reshape(-1, EXPERT_ROWS, 128)
        xt = _peer_apply(ids, gates, h2, xt, g_f, uv, seq)
    return xt.reshape(bsz, seq, d)
```

```python
import functools

import jax
import jax.numpy as jnp
from jax import lax
from jax.experimental import pallas as pl
from jax.experimental.pallas import tpu as pltpu

F32 = jnp.float32
BF16 = jnp.bfloat16
HI = lax.Precision.HIGHEST

D_MODEL = 1024
MLA_HEADS = 8
MLA_Q_RANK = 256
MLA_KV_RANK = 128
MLA_NOPE = 64
MLA_ROPE = 32
MLA_V = 64
MLA_QK = MLA_NOPE + MLA_ROPE
ROPE_THETA = 10000.0
HEADS = 8
HEAD_PAD = 128
GDN_DK = 64
GDN_DV = 64
GDN_CONV = 4
GDN_CHUNK = 64
MOBA_DH = 64
MOBA_BLOCK = 256
ATTN_BLOCK = MOBA_BLOCK
MOBA_TOPK = 3
N_BRANCH = 3
BRANCH_W = 512
PEER_HEADS = 8
PEER_NKEYS = 128
PEER_TOPK = 16
PEER_SEL = PEER_HEADS * PEER_TOPK
EXPERT_ROWS = 2 * D_MODEL // 128
NORM_EPS = 1e-6
NEG_INF = -1e30

COL_LAT = 0
COL_GQ = 512
COL_MV = 2560
COL_MQ = 3072
COL_GATE = 4096
IN_COLS_PAD = 7168

VMEM_LIMIT = 56 * 1024 * 1024


def _cp(sem, vmem=None):
    return pltpu.CompilerParams(dimension_semantics=sem, vmem_limit_bytes=vmem or VMEM_LIMIT)


def _nt(a, b, precision=None):
    return lax.dot_general(a, b, (((1,), (1,)), ((), ())), precision=precision, preferred_element_type=F32)


def _tn(a, b, precision=None):
    return lax.dot_general(a, b, (((0,), (0,)), ((), ())), precision=precision, preferred_element_type=F32)


def _mm(a, b, precision=None):
    return jnp.dot(a, b, precision=precision, preferred_element_type=F32)


def _silu(x):
    return x * jax.nn.sigmoid(x)


def _mod_kernel(c_ref, w_ref, b_ref, o_ref):
    c = c_ref[...]
    o_ref[0] = _mm(_silu(c), w_ref[0], HI) + b_ref[0]


def _mod_all(c, w_mod, b_mod):
    depth, d, n = w_mod.shape
    bsz = c.shape[0]
    tn = 1024
    return pl.pallas_call(
        _mod_kernel,
        grid=(depth, n // tn),
        in_specs=[pl.BlockSpec((bsz, d), lambda l, j: (0, 0)),
                  pl.BlockSpec((1, d, tn), lambda l, j: (l, 0, j)),
                  pl.BlockSpec((1, 1, tn), lambda l, j: (l, 0, j))],
        out_specs=pl.BlockSpec((1, bsz, tn), lambda l, j: (l, 0, j)),
        out_shape=jax.ShapeDtypeStruct((depth, bsz, n), F32),
        compiler_params=_cp(("arbitrary", "arbitrary")),
        name="adaln_mod",
    )(c, w_mod, b_mod.reshape(depth, 1, n))


def _modmm_kernel(x_ref, sh_ref, sc_ref, w_ref, wlo_ref, o_ref, h_ref, hb_scr, hlo_scr, *, lo_first, lo_tiles):
    j = pl.program_id(1)

    @pl.when(j == 0)
    def _():
        x = x_ref[...]
        h = x * lax.rsqrt(jnp.mean(x * x, axis=-1, keepdims=True) + NORM_EPS)
        h = h * (1.0 + sc_ref[0]) + sh_ref[0]
        h_ref[...] = h
        hb = h.astype(BF16)
        hb_scr[...] = hb
        hlo_scr[...] = (h - hb.astype(F32)).astype(BF16)

    precise = (j >= lo_first) & (j < lo_first + lo_tiles)

    @pl.when(precise)
    def _():
        o_ref[...] = (_mm(hb_scr[...], w_ref[...]) + _mm(hb_scr[...], wlo_ref[...])
                      + _mm(hlo_scr[...], w_ref[...]))

    @pl.when(jnp.logical_not(precise))
    def _():
        o_ref[...] = _mm(hb_scr[...], w_ref[...])


def _modulate_matmul(x, shift, scale, w_bf16, w_lo, lo_first, seq, tm=1024, tn=1024):
    t, d = x.shape
    n = w_bf16.shape[1]
    lo_tiles = w_lo.shape[1] // tn
    bsz = shift.shape[0]
    bidx = lambda i, j: ((i * tm) // seq, 0, 0)
    return pl.pallas_call(
        functools.partial(_modmm_kernel, lo_first=lo_first, lo_tiles=lo_tiles),
        grid=(t // tm, n // tn),
        in_specs=[pl.BlockSpec((tm, d), lambda i, j: (i, 0)),
                  pl.BlockSpec((1, 1, d), bidx),
                  pl.BlockSpec((1, 1, d), bidx),
                  pl.BlockSpec((d, tn), lambda i, j: (0, j)),
                  pl.BlockSpec((d, tn), lambda i, j: (0, jnp.clip(j - lo_first, 0, lo_tiles - 1)))],
        out_specs=[pl.BlockSpec((tm, tn), lambda i, j: (i, j)),
                   pl.BlockSpec((tm, d), lambda i, j: (i, 0))],
        out_shape=[jax.ShapeDtypeStruct((t, n), F32), jax.ShapeDtypeStruct((t, d), F32)],
        scratch_shapes=[pltpu.VMEM((tm, d), BF16), pltpu.VMEM((tm, d), BF16)],
        compiler_params=_cp(("arbitrary", "arbitrary")),
        name="modulate_matmul",
    )(x, shift.reshape(bsz, 1, d), scale.reshape(bsz, 1, d), w_bf16, w_lo)


def _mla_prep_kernel(p_ref, pos_ref, qg_ref, wuq_ref, kvg_ref, wuk_ref, wuv_ref, qng_ref, kng_ref, invf_ref,
                     q_out, k_out, v_out):
    tm = p_ref.shape[0]
    cq = p_ref[:, 0:MLA_Q_RANK]
    ckv = p_ref[:, MLA_Q_RANK:MLA_Q_RANK + MLA_KV_RANK]
    misc = p_ref[:, MLA_Q_RANK + MLA_KV_RANK:MLA_Q_RANK + MLA_KV_RANK + 128]

    def rms(v, n):
        return v * lax.rsqrt(jnp.sum(v * v, axis=-1, keepdims=True) * (1.0 / n) + NORM_EPS)

    qn = (rms(cq, MLA_Q_RANK) * qg_ref[...]).astype(BF16)
    q_all = _mm(qn, wuq_ref[...])
    kvn = (rms(ckv, MLA_KV_RANK) * kvg_ref[...]).astype(BF16)
    k_all = _mm(kvn, wuk_ref[...])
    v_out[0, 0] = _mm(kvn, wuv_ref[...]).T.astype(BF16)

    lane = lax.broadcasted_iota(jnp.int32, (tm, HEAD_PAD), 1)
    in_rope = (lane >= MLA_NOPE) & (lane < MLA_QK)
    k_rope = jnp.where(in_rope, pltpu.roll(misc, MLA_NOPE, 1), 0.0)
    ang = pos_ref[...].astype(F32) * invf_ref[...]
    cos = jnp.cos(ang)
    sin = jnp.sin(ang)
    half = MLA_ROPE // 2
    c_tab = jnp.where(lane < MLA_NOPE, 1.0, jnp.where(in_rope, cos, 0.0))
    s_lo = jnp.where(in_rope & (lane < MLA_NOPE + half), -sin, 0.0)
    s_hi = jnp.where(in_rope & (lane >= MLA_NOPE + half), sin, 0.0)

    def finish(xh, gain):
        xh = xh * lax.rsqrt(jnp.sum(xh * xh, axis=-1, keepdims=True) * (1.0 / MLA_QK) + NORM_EPS) * gain
        return xh * c_tab + pltpu.roll(xh, HEAD_PAD - half, 1) * s_lo + pltpu.roll(xh, half, 1) * s_hi

    scale = MLA_QK ** -0.5
    for h in range(MLA_HEADS):
        sl = slice(h * HEAD_PAD, (h + 1) * HEAD_PAD)
        q_out[0, 0, sl, :] = (finish(q_all[:, sl], qng_ref[...]) * scale).T.astype(BF16)
        k_out[:, sl] = finish(k_all[:, sl] + k_rope, kng_ref[...]).astype(BF16)


def _fm_spec(rows, tm, tiles_per_seq):
    return pl.BlockSpec((1, 1, rows, tm), lambda i: (i // tiles_per_seq, i % tiles_per_seq, 0, 0))


def _mla_prep(proj, pos, lw, seq):
    t = proj.shape[0]
    tm = ATTN_BLOCK
    nblk = seq // tm
    full = lambda shp: pl.BlockSpec(shp, lambda i: (0,) * len(shp))
    return pl.pallas_call(
        _mla_prep_kernel,
        grid=(t // tm,),
        in_specs=[pl.BlockSpec((tm, 512), lambda i: (i, 0)),
                  pl.BlockSpec((tm, 1), lambda i: (i, 0)),
                  full((1, MLA_Q_RANK)), full((MLA_Q_RANK, HEADS * HEAD_PAD)),
                  full((1, MLA_KV_RANK)), full((MLA_KV_RANK, HEADS * HEAD_PAD)),
                  full((MLA_KV_RANK, HEADS * MLA_V)),
                  full((1, HEAD_PAD)), full((1, HEAD_PAD)), full((1, HEAD_PAD))],
        out_specs=[_fm_spec(HEADS * HEAD_PAD, tm, nblk),
                   pl.BlockSpec((tm, HEADS * HEAD_PAD), lambda i: (i, 0)),
                   _fm_spec(HEADS * MLA_V, tm, nblk)],
        out_shape=[jax.ShapeDtypeStruct((t // seq, nblk, HEADS * HEAD_PAD, tm), BF16),
                   jax.ShapeDtypeStruct((t, HEADS * HEAD_PAD), BF16),
                   jax.ShapeDtypeStruct((t // seq, nblk, HEADS * MLA_V, tm), BF16)],
        compiler_params=_cp(("arbitrary",)),
        name="mla_prep",
    )(proj, pos, lw["mla_q_gain"], lw["mla_w_uq"], lw["mla_kv_gain"], lw["mla_w_uk"], lw["mla_w_uv"],
      lw["mla_qn_gain"], lw["mla_kn_gain"], lw["rope_inv_freq"])


def _attn_kernel(qt_ref, k_ref, vt_ref, o_ref, *, blk, hg):
    i = pl.program_id(2)
    half = blk // 2
    kidx = lax.broadcasted_iota(jnp.int32, (half, blk), 0)
    qidx = lax.broadcasted_iota(jnp.int32, (half, blk), 1)
    heads = range(hg)
    qts = [qt_ref[0, 0, hh * HEAD_PAD:(hh + 1) * HEAD_PAD, :] for hh in heads]

    chains = [(hh, kh) for hh in heads for kh in range(2)]
    n = len(chains)

    def scores(j):
        start = pl.multiple_of(j * blk, blk)
        rows = [pl.ds(pl.multiple_of(start + kh * half, half), half) for kh in range(2)]
        return [_mm(k_ref[rows[kh], hh * HEAD_PAD:(hh + 1) * HEAD_PAD], qts[hh]) for hh, kh in chains]

    def update(j, st, carry, masked):
        ms, ls, accs = carry
        if masked:
            st = [jnp.where(kidx + kh * half <= qidx, s, NEG_INF) for s, (hh, kh) in zip(st, chains)]
        m_new = [jnp.maximum(ms[c], jnp.max(st[c], axis=0, keepdims=True)) for c in range(n)]
        pt = [jnp.exp(st[c] - m_new[c]) for c in range(n)]
        alpha = [jnp.exp(ms[c] - m_new[c]) for c in range(n)]
        ls = [alpha[c] * ls[c] + jnp.sum(pt[c], axis=0, keepdims=True) for c in range(n)]
        accs = [alpha[c] * accs[c]
                + _mm(vt_ref[0, j, hh * MLA_V:(hh + 1) * MLA_V, kh * half:(kh + 1) * half], pt[c].astype(BF16))
                for c, (hh, kh) in enumerate(chains)]
        return tuple(m_new), tuple(ls), tuple(accs)

    def pair(p, carry):
        j = 2 * p
        st0 = scores(j)
        st1 = scores(j + 1)
        carry = update(j, st0, carry, False)
        return update(j + 1, st1, carry, False)

    def finish(carry):
        ms, ls, accs = carry
        outs = []
        for hh in heads:
            a, b = 2 * hh, 2 * hh + 1
            m = jnp.maximum(ms[a], ms[b])
            wa, wb = jnp.exp(ms[a] - m), jnp.exp(ms[b] - m)
            outs.append((wa * accs[a] + wb * accs[b]) / (wa * ls[a] + wb * ls[b]))
        o_ref[...] = jnp.concatenate(outs, axis=0).T

    init = (tuple(jnp.full((1, blk), -jnp.inf, F32) for _ in chains),
            tuple(jnp.zeros((1, blk), F32) for _ in chains),
            tuple(jnp.zeros((MLA_V, blk), F32) for _ in chains))
    carry = lax.fori_loop(0, i // 2, pair, init)

    @pl.when(i % 2 == 0)
    def _():
        finish(update(i, scores(i), carry, True))

    @pl.when(i % 2 == 1)
    def _():
        st0 = scores(i - 1)
        st1 = scores(i)
        finish(update(i, st1, update(i - 1, st0, carry, False), True))


def _causal_attention(qt, k, vt, bsz, seq, hg=4):
    t = k.shape[0]
    blk = ATTN_BLOCK
    nq = seq // blk
    return pl.pallas_call(
        functools.partial(_attn_kernel, blk=blk, hg=hg),
        grid=(bsz, HEADS // hg, nq),
        in_specs=[pl.BlockSpec((1, 1, hg * HEAD_PAD, blk), lambda b, g, i: (b, i, g, 0)),
                  pl.BlockSpec((seq, hg * HEAD_PAD), lambda b, g, i: (b, g)),
                  pl.BlockSpec((1, nq, hg * MLA_V, blk), lambda b, g, i: (b, 0, g, 0))],
        out_specs=pl.BlockSpec((blk, hg * MLA_V), lambda b, g, i: (b * nq + i, g)),
        out_shape=jax.ShapeDtypeStruct((t, HEADS * MLA_V), F32),
        compiler_params=_cp(("arbitrary", "arbitrary", "arbitrary")),
        name="causal_attention",
    )(qt, k, vt)


def _moba_prep_kernel(mq_ref, mk_ref, mv_ref, qg_ref, kg_ref, e_ref, et_ref, q_out, k_out, v_out, kmean_scr, *, nb):
    tm = mq_ref.shape[0]
    n = pl.program_id(0) % nb

    @pl.when(n == 0)
    def _():
        kmean_scr[...] = jnp.zeros_like(kmean_scr)

    def headnorm(x, gain):
        ss = _mm(x * x, e_ref[...], HI)
        inv = lax.rsqrt(ss * (1.0 / MOBA_DH) + NORM_EPS)
        return x * _mm(inv, et_ref[...], HI) * gain

    qn = headnorm(mq_ref[...], qg_ref[...])
    kn = headnorm(mk_ref[...], kg_ref[...])
    v_out[0, 0] = mv_ref[...].T.astype(BF16)
    kmean_scr[pl.ds(n, 1), :] = jnp.mean(kn, axis=0, keepdims=True)
    km = kmean_scr[...]

    lane = lax.broadcasted_iota(jnp.int32, (tm, nb), 1)
    zpad = jnp.zeros((tm, HEAD_PAD - MOBA_DH - nb), F32)
    onehot = jnp.where(lane == n, 1.0, 0.0)
    for h in range(HEADS):
        sl = slice(h * MOBA_DH, (h + 1) * MOBA_DH)
        gate = _nt(qn[:, sl], km[:, sl], HI)
        gate = jnp.where(lane < n, gate, -jnp.inf)
        pen = jnp.full((tm, nb), NEG_INF, F32)
        for r in range(MOBA_TOPK):
            mx = jnp.max(gate, axis=-1, keepdims=True)
            idx = jnp.min(jnp.where(gate == mx, lane, nb), axis=-1, keepdims=True)
            hit = (lane == idx) & (r < n)
            pen = jnp.where(hit, 0.0, pen)
            gate = jnp.where(lane == idx, -jnp.inf, gate)
        pen = jnp.where(lane == n, 0.0, pen)
        osl = slice(h * HEAD_PAD, (h + 1) * HEAD_PAD)
        q_aug = jnp.concatenate([qn[:, sl] * (MOBA_DH ** -0.5), pen, zpad], axis=1)
        q_out[0, 0, osl, :] = q_aug.T.astype(BF16)
        k_out[:, osl] = jnp.concatenate([kn[:, sl], onehot, zpad], axis=1).astype(BF16)


def _moba_prep(proj, lw, seq):
    t = proj.shape[0]
    tm = MOBA_BLOCK
    nb = seq // tm
    full = lambda shp: pl.BlockSpec(shp, lambda i: (0,) * len(shp))
    c0 = COL_MQ // 512
    return pl.pallas_call(
        functools.partial(_moba_prep_kernel, nb=nb),
        grid=(t // tm,),
        in_specs=[pl.BlockSpec((tm, 512), lambda i: (i, c0)),
                  pl.BlockSpec((tm, 512), lambda i: (i, c0 + 1)),
                  pl.BlockSpec((tm, 512), lambda i: (i, COL_MV // 512)),
                  full((1, 512)), full((1, 512)), full((512, HEADS)), full((HEADS, 512))],
        out_specs=[_fm_spec(HEADS * HEAD_PAD, tm, nb),
                   pl.BlockSpec((tm, HEADS * HEAD_PAD), lambda i: (i, 0)),
                   _fm_spec(HEADS * MOBA_DH, tm, nb)],
        out_shape=[jax.ShapeDtypeStruct((t // seq, nb, HEADS * HEAD_PAD, tm), BF16),
                   jax.ShapeDtypeStruct((t, HEADS * HEAD_PAD), BF16),
                   jax.ShapeDtypeStruct((t // seq, nb, HEADS * MOBA_DH, tm), BF16)],
        scratch_shapes=[pltpu.VMEM((nb, HEADS * MOBA_DH), F32)],
        compiler_params=_cp(("arbitrary",)),
        name="moba_prep",
    )(proj, proj, proj, lw["moba_qn_gain"], lw["moba_kn_gain"], lw["head_sum"], lw["head_expand"])


def _gdn_prep_kernel(gq_ref, gk_ref, gv_ref, misc_ref, wq_ref, wk_ref, wv_ref, alog_ref, dtb_ref, e_ref, et_ref,
                     q_out, k_out, v_out, gb_out, ext_scr, *, tiles_per_seq):
    tm = gq_ref.shape[0]
    pad = 8

    @pl.when(pl.program_id(0) % tiles_per_seq == 0)
    def _():
        ext_scr[:, 0:pad, :] = jnp.zeros((3, pad, ext_scr.shape[2]), F32)

    def conv_silu(s, x_ref, w_ref):
        ext_scr[s, pad:pad + tm, :] = x_ref[...]
        y = jnp.zeros(x_ref.shape, F32)
        for i in range(GDN_CONV):
            y = y + ext_scr[s, pl.ds(pad - (GDN_CONV - 1) + i, tm), :] * w_ref[i:i + 1, :]
        ext_scr[s, 0:pad, :] = ext_scr[s, tm:tm + pad, :]
        return _silu(y)

    def l2n(x):
        ss = _mm(x * x, e_ref[...], HI)
        return x * _mm(lax.rsqrt(ss + NORM_EPS), et_ref[...], HI)

    q_out[...] = l2n(conv_silu(0, gq_ref, wq_ref)) * (GDN_DK ** -0.5)
    k_out[...] = l2n(conv_silu(1, gk_ref, wk_ref))
    v_out[...] = conv_silu(2, gv_ref, wv_ref)
    misc = misc_ref[...]
    a = misc[:, MLA_ROPE:MLA_ROPE + HEADS]
    b = misc[:, MLA_ROPE + HEADS:MLA_ROPE + 2 * HEADS]
    g = -jnp.exp(alog_ref[...]) * jax.nn.softplus(a + dtb_ref[...])
    beta = jax.nn.sigmoid(b)
    gb_out[...] = jnp.concatenate([g, beta, jnp.zeros((tm, 128 - 2 * HEADS), F32)], axis=1)


def _gdn_prep(proj, lw, seq, tm=256):
    t = proj.shape[0]
    full = lambda shp: pl.BlockSpec(shp, lambda i: (0,) * len(shp))
    c0 = COL_GQ // 512
    tok = lambda w: pl.BlockSpec((tm, w), lambda i: (i, 0))
    return pl.pallas_call(
        functools.partial(_gdn_prep_kernel, tiles_per_seq=seq // tm),
        grid=(t // tm,),
        in_specs=[pl.BlockSpec((tm, 512), lambda i: (i, c0)),
                  pl.BlockSpec((tm, 512), lambda i: (i, c0 + 1)),
                  pl.BlockSpec((tm, 512), lambda i: (i, c0 + 2)),
                  pl.BlockSpec((tm, 128), lambda i: (i, 3)),
                  full((GDN_CONV, 512)), full((GDN_CONV, 512)), full((GDN_CONV, 512)),
                  full((1, HEADS)), full((1, HEADS)), full((512, HEADS)), full((HEADS, 512))],
        out_specs=[tok(512), tok(512), tok(512), tok(128)],
        out_shape=[jax.ShapeDtypeStruct((t, 512), F32)] * 3 + [jax.ShapeDtypeStruct((t, 128), F32)],
        scratch_shapes=[pltpu.VMEM((3, tm + 8, 512), F32)],
        compiler_params=_cp(("arbitrary",)),
        name="gdn_prep",
    )(proj, proj, proj, proj, lw["conv_wq"], lw["conv_wk"], lw["conv_wv"], lw["gdn_a_log"], lw["gdn_dt_bias"],
      lw["head_sum"], lw["head_expand"])


def _gdn_scan_kernel(q_ref, k_ref, v_ref, gb_ref, z_ref, og_ref, o_ref, s_scr, *, chunks):
    c = GDN_CHUNK

    @pl.when(pl.program_id(1) == 0)
    def _():
        s_scr[...] = jnp.zeros_like(s_scr)

    ri = lax.broadcasted_iota(jnp.int32, (c, c), 0)
    ci = lax.broadcasted_iota(jnp.int32, (c, c), 1)
    tri = ci <= ri
    strict = ci < ri
    ltri = jnp.where(tri, 1.0, 0.0)
    probs = [(cc, h) for cc in range(chunks) for h in range(HEADS)]
    rows = lambda cc: slice(cc * c, (cc + 1) * c)
    hsl = lambda h: slice(h * GDN_DK, (h + 1) * GDN_DK)
    gbs = [gb_ref[rows(cc), :] for cc in range(chunks)]
    gcs = [_mm(ltri, gb[:, 0:HEADS], HI) for gb in gbs]
    gcts = [g.T for g in gcs]
    q = [q_ref[rows(cc), hsl(h)] for cc, h in probs]
    k = [k_ref[rows(cc), hsl(h)] for cc, h in probs]
    v = [v_ref[rows(cc), hsl(h)] for cc, h in probs]
    gcol = [gcs[cc][:, h:h + 1] for cc, h in probs]
    grow = [gcts[cc][h:h + 1, :] for cc, h in probs]
    glast = [gcs[cc][c - 1:c, h:h + 1] for cc, h in probs]
    bcol = [gbs[cc][:, HEADS + h:HEADS + h + 1] for cc, h in probs]
    n = len(probs)
    egc = [jnp.exp(g) for g in gcol]
    decay = [jnp.where(tri, jnp.exp(jnp.where(tri, gcol[i] - grow[i], 0.0)), 0.0) for i in range(n)]
    kb = [k[i] * bcol[i] for i in range(n)]
    k16 = [a.astype(BF16) for a in k]
    kb16 = [a.astype(BF16) for a in kb]
    a_neg = [jnp.where(strict, -(_nt(kb16[i], k16[i]) * decay[i]), 0.0) for i in range(n)]
    x = [jnp.concatenate([v[i] * bcol[i], kb[i] * egc[i]], axis=1) for i in range(n)]
    def mm_split_rhs(a16_, rhs):
        hi = rhs.astype(BF16)
        lo = (rhs - hi.astype(F32)).astype(BF16)
        return _mm(a16_, hi) + _mm(a16_, lo)

    for lvl in range(6):
        a16 = [a.astype(BF16) for a in a_neg]
        if lvl < 5:
            r = [mm_split_rhs(a16[i], jnp.concatenate([a_neg[i], x[i]], axis=1)) for i in range(n)]
            a_neg = [ri_[:, 0:c] for ri_ in r]
            x = [x[i] + r[i][:, c:] for i in range(n)]
        else:
            x = [x[i] + mm_split_rhs(a16[i], x[i]) for i in range(n)]
    u = [xi[:, 0:GDN_DV] for xi in x]
    w16 = [xi[:, GDN_DV:].astype(BF16) for xi in x]
    attn16 = [(_nt(q[i].astype(BF16), k16[i]) * decay[i]).astype(BF16) for i in range(n)]
    qe16 = [(q[i] * egc[i]).astype(BF16) for i in range(n)]
    kd16 = [(k[i] * jnp.exp(glast[i] - gcol[i])).astype(BF16) for i in range(n)]
    eglast = [jnp.exp(g) for g in glast]
    state = [s_scr[h] for h in range(HEADS)]
    for cc in range(chunks):
        ids = [cc * HEADS + h for h in range(HEADS)]
        s16 = [s.astype(BF16) for s in state]
        v_new = [u[i] - _mm(w16[i], s16[h]) for h, i in enumerate(ids)]
        vn16 = [a.astype(BF16) for a in v_new]
        o = [_mm(qe16[i], s16[h]) + _mm(attn16[i], vn16[h]) for h, i in enumerate(ids)]
        state = [state[h] * eglast[i] + _tn(kd16[i], vn16[h]) for h, i in enumerate(ids)]
        o = [oh * lax.rsqrt(jnp.mean(oh * oh, axis=-1, keepdims=True) + NORM_EPS) * og_ref[...] for oh in o]
        o_ref[rows(cc), :] = jnp.concatenate(o, axis=1) * _silu(z_ref[rows(cc), :])
    for h in range(HEADS):
        s_scr[h] = state[h]


def _gdn_scan(qn, kn, vv, gb, proj, lw, bsz, seq, chunks=4):
    t = qn.shape[0]
    ct = chunks * GDN_CHUNK
    ns = seq // ct
    tok = lambda w: pl.BlockSpec((ct, w), lambda b, i: (b * ns + i, 0))
    return pl.pallas_call(
        functools.partial(_gdn_scan_kernel, chunks=chunks),
        grid=(bsz, ns),
        in_specs=[tok(512), tok(512), tok(512), tok(128),
                  pl.BlockSpec((ct, 512), lambda b, i: (b * ns + i, COL_GQ // 512 + 3)),
                  pl.BlockSpec((1, GDN_DV), lambda b, i: (0, 0))],
        out_specs=tok(512),
        out_shape=jax.ShapeDtypeStruct((t, 512), F32),
        scratch_shapes=[pltpu.VMEM((HEADS, GDN_DK, GDN_DV), F32)],
        compiler_params=_cp(("arbitrary", "arbitrary")),
        name="gdn_scan",
    )(qn, kn, vv, gb, proj, lw["gdn_o_gain"])


def _merge_kernel(oa_ref, ob_ref, oc_ref, g0_ref, g1_ref, g2_ref, x_ref, ga_ref, wb_ref, wo_ref, o_ref):
    y = None
    for o_n, g_n, n in ((oa_ref, g0_ref, 0), (ob_ref, g1_ref, 1), (oc_ref, g2_ref, 2)):
        term = jax.nn.sigmoid(g_n[...]) * _mm(o_n[...].astype(BF16), wb_ref[n])
        y = term if y is None else y + term
    o_ref[...] = x_ref[...] + ga_ref[0] * _mm(y.astype(BF16), wo_ref[...])


def _merge(o_mla, o_gdn, o_moba, proj, x, g_a, lw, seq, tm=512):
    t, d = x.shape
    bsz = g_a.shape[0]
    tok = lambda w: pl.BlockSpec((tm, w), lambda i: (i, 0))
    gcol = COL_GATE // d
    return pl.pallas_call(
        _merge_kernel,
        grid=(t // tm,),
        in_specs=[tok(BRANCH_W), tok(BRANCH_W), tok(BRANCH_W),
                  pl.BlockSpec((tm, d), lambda i: (i, gcol)),
                  pl.BlockSpec((tm, d), lambda i: (i, gcol + 1)),
                  pl.BlockSpec((tm, d), lambda i: (i, gcol + 2)),
                  tok(d),
                  pl.BlockSpec((1, 1, d), lambda i: ((i * tm) // seq, 0, 0)),
                  pl.BlockSpec((N_BRANCH, BRANCH_W, d), lambda i: (0, 0, 0)),
                  pl.BlockSpec((d, d), lambda i: (0, 0))],
        out_specs=tok(d),
        out_shape=jax.ShapeDtypeStruct((t, d), F32),
        compiler_params=_cp(("arbitrary",)),
        name="branch_merge",
    )(o_mla, o_gdn, o_moba, proj, proj, proj, x, g_a.reshape(bsz, 1, d), lw["w_branch"], lw["w_out"])


def _topk_rows(s, k):
    n = s.shape[0]
    ri = lax.broadcasted_iota(jnp.int32, s.shape, 0).astype(F32)
    vals, idxs = [], []
    for _ in range(k):
        mx = jnp.max(s, axis=0, keepdims=True)
        idx = jnp.min(jnp.where(s == mx, ri, float(n)), axis=0, keepdims=True)
        vals.append(mx)
        idxs.append(idx)
        s = jnp.where(ri == idx, -jnp.inf, s)
    return jnp.concatenate(vals, axis=0), jnp.concatenate(idxs, axis=0).astype(jnp.int32)


def _peer_select_kernel(qry_ref, keys_ref, ids_out, gates_out):
    kk = PEER_TOPK
    ids_rows, gate_rows = [], []
    for h in range(PEER_HEADS):
        halves = []
        for p in range(2):
            g = h * 2 + p
            s_t = _nt(keys_ref[g], qry_ref[:, g * 128:(g + 1) * 128], HI)
            halves.append(_topk_rows(s_t, kk))
        (v1, i1), (v2, i2) = halves
        cand, a_of, b_of = [], [], []
        for a in range(kk):
            nb = kk // (a + 1)
            cand.append(v1[a:a + 1, :] + v2[0:nb, :])
            a_of += [a] * nb
            b_of += list(range(nb))
        cand = jnp.concatenate(cand, axis=0)
        top_s, top_r = _topk_rows(cand, kk)
        a_sel = jnp.zeros_like(top_r)
        b_sel = jnp.zeros_like(top_r)
        for r, (a, b) in enumerate(zip(a_of, b_of)):
            hit = top_r == r
            a_sel = jnp.where(hit, a, a_sel)
            b_sel = jnp.where(hit, b, b_sel)
        e1 = jnp.zeros_like(top_r)
        e2 = jnp.zeros_like(top_r)
        for a in range(kk):
            e1 = jnp.where(a_sel == a, i1[a:a + 1, :], e1)
            e2 = jnp.where(b_sel == a, i2[a:a + 1, :], e2)
        ids_rows.append(e1 * PEER_NKEYS + e2)
        ex = jnp.exp(top_s - top_s[0:1, :])
        gate_rows.append(ex / jnp.sum(ex, axis=0, keepdims=True))
    ids_out[...] = jnp.concatenate(ids_rows, axis=0).T
    gates_out[...] = jnp.concatenate(gate_rows, axis=0).T


def _peer_select(qry, sub_keys, tm=256):
    t = qry.shape[0]
    return pl.pallas_call(
        _peer_select_kernel,
        grid=(t // tm,),
        in_specs=[pl.BlockSpec((tm, 2 * PEER_HEADS * 128), lambda i: (i, 0)),
                  pl.BlockSpec((2 * PEER_HEADS, PEER_NKEYS, 128), lambda i: (0, 0, 0))],
        out_specs=[pl.BlockSpec((tm, PEER_SEL), lambda i: (i, 0)),
                   pl.BlockSpec((tm, PEER_SEL), lambda i: (i, 0))],
        out_shape=[jax.ShapeDtypeStruct((t, PEER_SEL), jnp.int32), jax.ShapeDtypeStruct((t, PEER_SEL), F32)],
        compiler_params=_cp(("arbitrary",)),
        name="peer_select",
    )(qry, sub_keys)


def _peer_apply_kernel(ids_ref, gates_ref, h_ref, x_ref, gf_ref, uv_hbm, o_ref, buf0, buf1, sems, *, tt):
    s = pl.program_id(0)
    ns = pl.num_programs(0) - 1
    slot = s % 2
    buf = (buf0, buf1)

    def start_rows(dst_slot):
        for t in range(tt):
            for e in range(PEER_SEL):
                pltpu.make_async_copy(uv_hbm.at[ids_ref[t, e]], buf[dst_slot].at[t, e],
                                      sems.at[dst_slot]).start(priority=e % 2)

    def wait_rows(src_slot):
        pltpu.make_async_copy(buf[1 - src_slot], buf[src_slot], sems.at[src_slot]).wait()

    def reduce_tile(src_slot):
        src = buf[src_slot]
        nsub = EXPERT_ROWS // 2
        prod = src[:, :, 0:nsub, :] * h_ref[...][:, None, :, :]
        act = jnp.sum(jnp.sum(prod, axis=2), axis=-1)
        gel = 0.5 * act * (1.0 + lax.erf(act * (2.0 ** -0.5)))
        wgt = (gates_ref[...] * gel)[:, :, None, None]
        out = jnp.sum(wgt * src[:, :, nsub:, :], axis=1)
        o_ref[...] = x_ref[...] + gf_ref[...] * out

    def steady(parity):
        wait_rows(1 - parity)
        start_rows(parity)
        reduce_tile(1 - parity)

    def last(parity):
        wait_rows(1 - parity)
        reduce_tile(1 - parity)

    for parity in range(2):
        pl.when((s >= 1) & (s < ns) & (slot == parity))(functools.partial(steady, parity))
    pl.when(s == 0)(functools.partial(start_rows, 0))
    for parity in range(2):
        pl.when((s == ns) & (slot == parity))(functools.partial(last, parity))


def _peer_apply(ids, gates, h2, x, g_f, uv, seq, tt=16):
    t, d = x.shape
    bsz = g_f.shape[0]
    ns = t // tt
    sub = d // 128
    cur = lambda i: jnp.maximum(i - 1, 0)
    tile3 = pl.BlockSpec((tt, sub, 128), lambda i: (cur(i), 0, 0))
    out = pl.pallas_call(
        functools.partial(_peer_apply_kernel, tt=tt),
        grid=(ns + 1,),
        in_specs=[pl.BlockSpec((tt, PEER_SEL), lambda i: (jnp.minimum(i, ns - 1), 0), memory_space=pltpu.SMEM),
                  pl.BlockSpec((tt, PEER_SEL), lambda i: (cur(i), 0)), tile3, tile3,
                  pl.BlockSpec((1, sub, 128), lambda i: ((cur(i) * tt) // seq, 0, 0)),
                  pl.BlockSpec(memory_space=pl.ANY)],
        out_specs=tile3,
        out_shape=jax.ShapeDtypeStruct((t, sub, 128), F32),
        scratch_shapes=[pltpu.VMEM((tt, PEER_SEL, EXPERT_ROWS, 128), F32),
                        pltpu.VMEM((tt, PEER_SEL, EXPERT_ROWS, 128), F32),
                        pltpu.SemaphoreType.DMA((2,))],
        compiler_params=_cp(("arbitrary",)),
        name="peer_apply",
    )(ids, gates, h2.reshape(t, sub, 128), x.reshape(t, sub, 128), g_f.reshape(bsz, sub, 128), uv)
    return out.reshape(t, d)


def _prep_layer_weights(l, w_in, mla_q_gain, mla_w_uq, mla_kv_gain, mla_w_ukv, mla_qn_gain, mla_kn_gain,
                        gdn_conv_w, gdn_a_log, gdn_dt_bias, gdn_o_gain, moba_qn_gain, moba_kn_gain,
                        w_branch, w_out, peer_w_query, peer_sub_keys):
    d = D_MODEL
    w = w_in[l]
    o = 0
    parts = {}
    for name, width in (("cq", 256), ("ckv", 128), ("kr", 32), ("gq", 512), ("gk", 512), ("gv", 512), ("gz", 512),
                        ("ga", 8), ("gb", 8), ("mq", 512), ("mk", 512), ("mv", 512), ("gate", 3072)):
        parts[name] = w[:, o:o + width]
        o += width
    w_in_p = jnp.concatenate(
        [parts["cq"], parts["ckv"], parts["kr"], parts["ga"], parts["gb"], jnp.zeros((d, 80), F32),
         parts["gq"], parts["gk"], parts["gv"], parts["gz"], parts["mv"], parts["mq"], parts["mk"], parts["gate"]],
        axis=1)
    w_in_hi = w_in_p.astype(BF16)
    sel = slice(COL_MQ, COL_MQ + 1024)
    w_in_lo = (w_in_p[:, sel] - w_in_hi[:, sel].astype(F32)).astype(BF16)
    wq_hi = peer_w_query[l].astype(BF16)
    wq_lo = (peer_w_query[l] - wq_hi.astype(F32)).astype(BF16)

    def pad_heads(m, width):
        r = m.shape[0]
        return jnp.pad(m.reshape(r, HEADS, width), ((0, 0), (0, 0), (0, HEAD_PAD - width))).reshape(r, HEADS * HEAD_PAD)

    ukv = mla_w_ukv[l].reshape(MLA_KV_RANK, HEADS, MLA_NOPE + MLA_V)
    pad1 = lambda g, n: jnp.pad(g, (0, n - g.shape[0])).reshape(1, n)
    lane = jnp.arange(HEAD_PAD)
    half = MLA_ROPE // 2
    inv_freq = jnp.where((lane >= MLA_NOPE) & (lane < MLA_QK),
                         ROPE_THETA ** (-((lane - MLA_NOPE) % half).astype(F32) / half), 0.0).reshape(1, HEAD_PAD)
    head_sum = jnp.repeat(jnp.eye(HEADS, dtype=F32), 64, axis=0)
    return dict(
        w_in=w_in_hi, w_in_lo=w_in_lo,
        mla_q_gain=mla_q_gain[l].reshape(1, -1),
        mla_w_uq=pad_heads(mla_w_uq[l], MLA_QK).astype(BF16),
        mla_kv_gain=mla_kv_gain[l].reshape(1, -1),
        mla_w_uk=pad_heads(ukv[:, :, :MLA_NOPE].reshape(MLA_KV_RANK, -1), MLA_NOPE).astype(BF16),
        mla_w_uv=ukv[:, :, MLA_NOPE:].reshape(MLA_KV_RANK, -1).astype(BF16),
        mla_qn_gain=pad1(mla_qn_gain[l], HEAD_PAD),
        mla_kn_gain=pad1(mla_kn_gain[l], HEAD_PAD),
        rope_inv_freq=inv_freq,
        conv_wq=gdn_conv_w[l][:, 0:512], conv_wk=gdn_conv_w[l][:, 512:1024], conv_wv=gdn_conv_w[l][:, 1024:1536],
        gdn_a_log=gdn_a_log[l].reshape(1, -1), gdn_dt_bias=gdn_dt_bias[l].reshape(1, -1),
        gdn_o_gain=gdn_o_gain[l].reshape(1, -1),
        moba_qn_gain=jnp.tile(moba_qn_gain[l], HEADS).reshape(1, -1),
        moba_kn_gain=jnp.tile(moba_kn_gain[l], HEADS).reshape(1, -1),
        head_sum=head_sum, head_expand=head_sum.T,
        w_branch=w_branch[l].astype(BF16), w_out=w_out[l].astype(BF16),
        peer_w_query=wq_hi, peer_w_query_lo=wq_lo,
        peer_sub_keys=peer_sub_keys[l].reshape(2 * PEER_HEADS, PEER_NKEYS, -1),
    )


def kernel(x, c, positions, w_mod, b_mod, w_in, mla_q_gain, mla_w_uq, mla_kv_gain, mla_w_ukv, mla_qn_gain, mla_kn_gain, gdn_conv_w, gdn_a_log, gdn_dt_bias, gdn_o_gain, moba_qn_gain, moba_kn_gain, w_branch, w_out, peer_w_query, peer_sub_keys, peer_u, peer_v):
    bsz, seq, d = x.shape
    t = bsz * seq
    depth = w_mod.shape[0]
    xt = x.reshape(t, d)
    pos = positions.reshape(t, 1)
    mod = _mod_all(c, w_mod, b_mod)
    for l in range(depth):
        lw = _prep_layer_weights(l, w_in, mla_q_gain, mla_w_uq, mla_kv_gain, mla_w_ukv, mla_qn_gain, mla_kn_gain,
                                 gdn_conv_w, gdn_a_log, gdn_dt_bias, gdn_o_gain, moba_qn_gain, moba_kn_gain,
                                 w_branch, w_out, peer_w_query, peer_sub_keys)
        sh_a, sc_a, g_a, sh_f, sc_f, g_f = [mod[l, :, i * d:(i + 1) * d] for i in range(6)]
        proj, _ = _modulate_matmul(xt, sh_a, sc_a, lw["w_in"], lw["w_in_lo"], COL_MQ // 1024, seq)
        q, k, v = _mla_prep(proj, pos, lw, seq)
        o_mla = _causal_attention(q, k, v, bsz, seq)
        q, k, v = _moba_prep(proj, lw, seq)
        o_moba = _causal_attention(q, k, v, bsz, seq)
        qn, kn, vv, gb = _gdn_prep(proj, lw, seq)
        o_gdn = _gdn_scan(qn, kn, vv, gb, proj, lw, bsz, seq)
        xt = _merge(o_mla, o_gdn, o_moba, proj, xt, g_a, lw, seq)
        qry, h2 = _modulate_matmul(xt, sh_f, sc_f, lw["peer_w_query"], lw["peer_w_query_lo"], 0, seq)
        ids, gates = _peer_select(qry, lw["peer_sub_keys"])
        uv = jnp.concatenate([peer_u[l], peer_v[l]], axis=1).reshape(-1, EXPERT_ROWS, 128)
        xt = _peer_apply(ids, gates, h2, xt, g_f, uv, seq)
    return xt.reshape(bsz, seq, d)
```

```python
import functools

import jax
import jax.numpy as jnp
from jax import lax
from jax.experimental import pallas as pl
from jax.experimental.pallas import tpu as pltpu

F32 = jnp.float32
BF16 = jnp.bfloat16
HI = lax.Precision.HIGHEST

D_MODEL = 1024
MLA_HEADS = 8
MLA_Q_RANK = 256
MLA_KV_RANK = 128
MLA_NOPE = 64
MLA_ROPE = 32
MLA_V = 64
MLA_QK = MLA_NOPE + MLA_ROPE
ROPE_THETA = 10000.0
HEADS = 8
HEAD_PAD = 128
GDN_DK = 64
GDN_DV = 64
GDN_CONV = 4
GDN_CHUNK = 64
MOBA_DH = 64
MOBA_BLOCK = 256
ATTN_BLOCK = MOBA_BLOCK
MOBA_TOPK = 3
N_BRANCH = 3
BRANCH_W = 512
PEER_HEADS = 8
PEER_NKEYS = 128
PEER_TOPK = 16
PEER_SEL = PEER_HEADS * PEER_TOPK
EXPERT_ROWS = 2 * D_MODEL // 128
NORM_EPS = 1e-6
NEG_INF = -1e30

COL_LAT = 0
COL_GQ = 512
COL_MV = 2560
COL_MQ = 3072
COL_GATE = 4096
IN_COLS_PAD = 7168

VMEM_LIMIT = 56 * 1024 * 1024


def _cp(sem, vmem=None):
    return pltpu.CompilerParams(dimension_semantics=sem, vmem_limit_bytes=vmem or VMEM_LIMIT)


def _nt(a, b, precision=None):
    return lax.dot_general(a, b, (((1,), (1,)), ((), ())), precision=precision, preferred_element_type=F32)


def _tn(a, b, precision=None):
    return lax.dot_general(a, b, (((0,), (0,)), ((), ())), precision=precision, preferred_element_type=F32)


def _mm(a, b, precision=None):
    return jnp.dot(a, b, precision=precision, preferred_element_type=F32)


def _silu(x):
    return x * jax.nn.sigmoid(x)


def _mod_kernel(c_ref, w_ref, b_ref, o_ref):
    c = c_ref[...]
    o_ref[0] = _mm(_silu(c), w_ref[0], HI) + b_ref[0]


def _mod_all(c, w_mod, b_mod):
    depth, d, n = w_mod.shape
    bsz = c.shape[0]
    tn = 1024
    return pl.pallas_call(
        _mod_kernel,
        grid=(depth, n // tn),
        in_specs=[pl.BlockSpec((bsz, d), lambda l, j: (0, 0)),
                  pl.BlockSpec((1, d, tn), lambda l, j: (l, 0, j)),
                  pl.BlockSpec((1, 1, tn), lambda l, j: (l, 0, j))],
        out_specs=pl.BlockSpec((1, bsz, tn), lambda l, j: (l, 0, j)),
        out_shape=jax.ShapeDtypeStruct((depth, bsz, n), F32),
        compiler_params=_cp(("arbitrary", "arbitrary")),
        name="adaln_mod",
    )(c, w_mod, b_mod.reshape(depth, 1, n))


def _modmm_kernel(x_ref, sh_ref, sc_ref, w_ref, wlo_ref, o_ref, h_ref, hb_scr, hlo_scr, *, lo_first, lo_tiles):
    j = pl.program_id(1)

    @pl.when(j == 0)
    def _():
        x = x_ref[...]
        h = x * lax.rsqrt(jnp.mean(x * x, axis=-1, keepdims=True) + NORM_EPS)
        h = h * (1.0 + sc_ref[0]) + sh_ref[0]
        h_ref[...] = h
        hb = h.astype(BF16)
        hb_scr[...] = hb
        hlo_scr[...] = (h - hb.astype(F32)).astype(BF16)

    precise = (j >= lo_first) & (j < lo_first + lo_tiles)

    @pl.when(precise)
    def _():
        o_ref[...] = (_mm(hb_scr[...], w_ref[...]) + _mm(hb_scr[...], wlo_ref[...])
                      + _mm(hlo_scr[...], w_ref[...]))

    @pl.when(jnp.logical_not(precise))
    def _():
        o_ref[...] = _mm(hb_scr[...], w_ref[...])


def _modulate_matmul(x, shift, scale, w_bf16, w_lo, lo_first, seq, tm=1024, tn=1024):
    t, d = x.shape
    n = w_bf16.shape[1]
    lo_tiles = w_lo.shape[1] // tn
    bsz = shift.shape[0]
    bidx = lambda i, j: ((i * tm) // seq, 0, 0)
    return pl.pallas_call(
        functools.partial(_modmm_kernel, lo_first=lo_first, lo_tiles=lo_tiles),
        grid=(t // tm, n // tn),
        in_specs=[pl.BlockSpec((tm, d), lambda i, j: (i, 0)),
                  pl.BlockSpec((1, 1, d), bidx),
                  pl.BlockSpec((1, 1, d), bidx),
                  pl.BlockSpec((d, tn), lambda i, j: (0, j)),
                  pl.BlockSpec((d, tn), lambda i, j: (0, jnp.clip(j - lo_first, 0, lo_tiles - 1)))],
        out_specs=[pl.BlockSpec((tm, tn), lambda i, j: (i, j)),
                   pl.BlockSpec((tm, d), lambda i, j: (i, 0))],
        out_shape=[jax.ShapeDtypeStruct((t, n), F32), jax.ShapeDtypeStruct((t, d), F32)],
        scratch_shapes=[pltpu.VMEM((tm, d), BF16), pltpu.VMEM((tm, d), BF16)],
        compiler_params=_cp(("arbitrary", "arbitrary")),
        name="modulate_matmul",
    )(x, shift.reshape(bsz, 1, d), scale.reshape(bsz, 1, d), w_bf16, w_lo)


def _mla_prep_kernel(p_ref, pos_ref, qg_ref, wuq_ref, kvg_ref, wuk_ref, wuv_ref, qng_ref, kng_ref, invf_ref,
                     q_out, k_out, v_out):
    tm = p_ref.shape[0]
    cq = p_ref[:, 0:MLA_Q_RANK]
    ckv = p_ref[:, MLA_Q_RANK:MLA_Q_RANK + MLA_KV_RANK]
    misc = p_ref[:, MLA_Q_RANK + MLA_KV_RANK:MLA_Q_RANK + MLA_KV_RANK + 128]

    def rms(v, n):
        return v * lax.rsqrt(jnp.sum(v * v, axis=-1, keepdims=True) * (1.0 / n) + NORM_EPS)

    qn = (rms(cq, MLA_Q_RANK) * qg_ref[...]).astype(BF16)
    q_all = _mm(qn, wuq_ref[...])
    kvn = (rms(ckv, MLA_KV_RANK) * kvg_ref[...]).astype(BF16)
    k_all = _mm(kvn, wuk_ref[...])
    v_out[0, 0] = _mm(kvn, wuv_ref[...]).T.astype(BF16)

    lane = lax.broadcasted_iota(jnp.int32, (tm, HEAD_PAD), 1)
    in_rope = (lane >= MLA_NOPE) & (lane < MLA_QK)
    k_rope = jnp.where(in_rope, pltpu.roll(misc, MLA_NOPE, 1), 0.0)
    ang = pos_ref[...].astype(F32) * invf_ref[...]
    cos = jnp.cos(ang)
    sin = jnp.sin(ang)
    half = MLA_ROPE // 2
    c_tab = jnp.where(lane < MLA_NOPE, 1.0, jnp.where(in_rope, cos, 0.0))
    s_lo = jnp.where(in_rope & (lane < MLA_NOPE + half), -sin, 0.0)
    s_hi = jnp.where(in_rope & (lane >= MLA_NOPE + half), sin, 0.0)

    def finish(xh, gain):
        xh = xh * lax.rsqrt(jnp.sum(xh * xh, axis=-1, keepdims=True) * (1.0 / MLA_QK) + NORM_EPS) * gain
        return xh * c_tab + pltpu.roll(xh, HEAD_PAD - half, 1) * s_lo + pltpu.roll(xh, half, 1) * s_hi

    scale = MLA_QK ** -0.5
    for h in range(MLA_HEADS):
        sl = slice(h * HEAD_PAD, (h + 1) * HEAD_PAD)
        q_out[0, 0, sl, :] = (finish(q_all[:, sl], qng_ref[...]) * scale).T.astype(BF16)
        k_out[:, sl] = finish(k_all[:, sl] + k_rope, kng_ref[...]).astype(BF16)


def _fm_spec(rows, tm, tiles_per_seq):
    return pl.BlockSpec((1, 1, rows, tm), lambda i: (i // tiles_per_seq, i % tiles_per_seq, 0, 0))


def _mla_prep(proj, pos, lw, seq):
    t = proj.shape[0]
    tm = ATTN_BLOCK
    nblk = seq // tm
    full = lambda shp: pl.BlockSpec(shp, lambda i: (0,) * len(shp))
    return pl.pallas_call(
        _mla_prep_kernel,
        grid=(t // tm,),
        in_specs=[pl.BlockSpec((tm, 512), lambda i: (i, 0)),
                  pl.BlockSpec((tm, 1), lambda i: (i, 0)),
                  full((1, MLA_Q_RANK)), full((MLA_Q_RANK, HEADS * HEAD_PAD)),
                  full((1, MLA_KV_RANK)), full((MLA_KV_RANK, HEADS * HEAD_PAD)),
                  full((MLA_KV_RANK, HEADS * MLA_V)),
                  full((1, HEAD_PAD)), full((1, HEAD_PAD)), full((1, HEAD_PAD))],
        out_specs=[_fm_spec(HEADS * HEAD_PAD, tm, nblk),
                   pl.BlockSpec((tm, HEADS * HEAD_PAD), lambda i: (i, 0)),
                   _fm_spec(HEADS * MLA_V, tm, nblk)],
        out_shape=[jax.ShapeDtypeStruct((t // seq, nblk, HEADS * HEAD_PAD, tm), BF16),
                   jax.ShapeDtypeStruct((t, HEADS * HEAD_PAD), BF16),
                   jax.ShapeDtypeStruct((t // seq, nblk, HEADS * MLA_V, tm), BF16)],
        compiler_params=_cp(("arbitrary",)),
        name="mla_prep",
    )(proj, pos, lw["mla_q_gain"], lw["mla_w_uq"], lw["mla_kv_gain"], lw["mla_w_uk"], lw["mla_w_uv"],
      lw["mla_qn_gain"], lw["mla_kn_gain"], lw["rope_inv_freq"])


def _attn_kernel(qt_ref, k_ref, vt_ref, o_ref, *, blk, hg):
    i = pl.program_id(2)
    half = blk // 2
    kidx = lax.broadcasted_iota(jnp.int32, (half, blk), 0)
    qidx = lax.broadcasted_iota(jnp.int32, (half, blk), 1)
    heads = range(hg)
    qts = [qt_ref[0, 0, hh * HEAD_PAD:(hh + 1) * HEAD_PAD, :] for hh in heads]

    chains = [(hh, kh) for hh in heads for kh in range(2)]
    n = len(chains)

    def scores(j):
        start = pl.multiple_of(j * blk, blk)
        rows = [pl.ds(pl.multiple_of(start + kh * half, half), half) for kh in range(2)]
        return [_mm(k_ref[rows[kh], hh * HEAD_PAD:(hh + 1) * HEAD_PAD], qts[hh]) for hh, kh in chains]

    def update(j, st, carry, masked):
        ms, ls, accs = carry
        if masked:
            st = [jnp.where(kidx + kh * half <= qidx, s, NEG_INF) for s, (hh, kh) in zip(st, chains)]
        m_new = [jnp.maximum(ms[c], jnp.max(st[c], axis=0, keepdims=True)) for c in range(n)]
        pt = [jnp.exp(st[c] - m_new[c]) for c in range(n)]
        alpha = [jnp.exp(ms[c] - m_new[c]) for c in range(n)]
        ls = [alpha[c] * ls[c] + jnp.sum(pt[c], axis=0, keepdims=True) for c in range(n)]
        accs = [alpha[c] * accs[c]
                + _mm(vt_ref[0, j, hh * MLA_V:(hh + 1) * MLA_V, kh * half:(kh + 1) * half], pt[c].astype(BF16))
                for c, (hh, kh) in enumerate(chains)]
        return tuple(m_new), tuple(ls), tuple(accs)

    def pair(p, carry):
        j = 2 * p
        st0 = scores(j)
        st1 = scores(j + 1)
        carry = update(j, st0, carry, False)
        return update(j + 1, st1, carry, False)

    def finish(carry):
        ms, ls, accs = carry
        outs = []
        for hh in heads:
            a, b = 2 * hh, 2 * hh + 1
            m = jnp.maximum(ms[a], ms[b])
            wa, wb = jnp.exp(ms[a] - m), jnp.exp(ms[b] - m)
            outs.append((wa * accs[a] + wb * accs[b]) / (wa * ls[a] + wb * ls[b]))
        o_ref[...] = jnp.concatenate(outs, axis=0).T

    init = (tuple(jnp.full((1, blk), -jnp.inf, F32) for _ in chains),
            tuple(jnp.zeros((1, blk), F32) for _ in chains),
            tuple(jnp.zeros((MLA_V, blk), F32) for _ in chains))
    carry = lax.fori_loop(0, i // 2, pair, init)

    @pl.when(i % 2 == 0)
    def _():
        finish(update(i, scores(i), carry, True))

    @pl.when(i % 2 == 1)
    def _():
        st0 = scores(i - 1)
        st1 = scores(i)
        finish(update(i, st1, update(i - 1, st0, carry, False), True))


def _causal_attention(qt, k, vt, bsz, seq, hg=4):
    t = k.shape[0]
    blk = ATTN_BLOCK
    nq = seq // blk
    return pl.pallas_call(
        functools.partial(_attn_kernel, blk=blk, hg=hg),
        grid=(bsz, HEADS // hg, nq),
        in_specs=[pl.BlockSpec((1, 1, hg * HEAD_PAD, blk), lambda b, g, i: (b, i, g, 0)),
                  pl.BlockSpec((seq, hg * HEAD_PAD), lambda b, g, i: (b, g)),
                  pl.BlockSpec((1, nq, hg * MLA_V, blk), lambda b, g, i: (b, 0, g, 0))],
        out_specs=pl.BlockSpec((blk, hg * MLA_V), lambda b, g, i: (b * nq + i, g)),
        out_shape=jax.ShapeDtypeStruct((t, HEADS * MLA_V), F32),
        compiler_params=_cp(("arbitrary", "arbitrary", "arbitrary")),
        name="causal_attention",
    )(qt, k, vt)


def _moba_prep_kernel(mq_ref, mk_ref, mv_ref, qg_ref, kg_ref, e_ref, et_ref, q_out, k_out, v_out, kmean_scr, *, nb):
    tm = mq_ref.shape[0]
    n = pl.program_id(0) % nb

    @pl.when(n == 0)
    def _():
        kmean_scr[...] = jnp.zeros_like(kmean_scr)

    def headnorm(x, gain):
        ss = _mm(x * x, e_ref[...], HI)
        inv = lax.rsqrt(ss * (1.0 / MOBA_DH) + NORM_EPS)
        return x * _mm(inv, et_ref[...], HI) * gain

    qn = headnorm(mq_ref[...], qg_ref[...])
    kn = headnorm(mk_ref[...], kg_ref[...])
    v_out[0, 0] = mv_ref[...].T.astype(BF16)
    kmean_scr[pl.ds(n, 1), :] = jnp.mean(kn, axis=0, keepdims=True)
    km = kmean_scr[...]

    lane = lax.broadcasted_iota(jnp.int32, (tm, nb), 1)
    zpad = jnp.zeros((tm, HEAD_PAD - MOBA_DH - nb), F32)
    onehot = jnp.where(lane == n, 1.0, 0.0)
    for h in range(HEADS):
        sl = slice(h * MOBA_DH, (h + 1) * MOBA_DH)
        gate = _nt(qn[:, sl], km[:, sl], HI)
        gate = jnp.where(lane < n, gate, -jnp.inf)
        pen = jnp.full((tm, nb), NEG_INF, F32)
        for r in range(MOBA_TOPK):
            mx = jnp.max(gate, axis=-1, keepdims=True)
            idx = jnp.min(jnp.where(gate == mx, lane, nb), axis=-1, keepdims=True)
            hit = (lane == idx) & (r < n)
            pen = jnp.where(hit, 0.0, pen)
            gate = jnp.where(lane == idx, -jnp.inf, gate)
        pen = jnp.where(lane == n, 0.0, pen)
        osl = slice(h * HEAD_PAD, (h + 1) * HEAD_PAD)
        q_aug = jnp.concatenate([qn[:, sl] * (MOBA_DH ** -0.5), pen, zpad], axis=1)
        q_out[0, 0, osl, :] = q_aug.T.astype(BF16)
        k_out[:, osl] = jnp.concatenate([kn[:, sl], onehot, zpad], axis=1).astype(BF16)


def _moba_prep(proj, lw, seq):
    t = proj.shape[0]
    tm = MOBA_BLOCK
    nb = seq // tm
    full = lambda shp: pl.BlockSpec(shp, lambda i: (0,) * len(shp))
    c0 = COL_MQ // 512
    return pl.pallas_call(
        functools.partial(_moba_prep_kernel, nb=nb),
        grid=(t // tm,),
        in_specs=[pl.BlockSpec((tm, 512), lambda i: (i, c0)),
                  pl.BlockSpec((tm, 512), lambda i: (i, c0 + 1)),
                  pl.BlockSpec((tm, 512), lambda i: (i, COL_MV // 512)),
                  full((1, 512)), full((1, 512)), full((512, HEADS)), full((HEADS, 512))],
        out_specs=[_fm_spec(HEADS * HEAD_PAD, tm, nb),
                   pl.BlockSpec((tm, HEADS * HEAD_PAD), lambda i: (i, 0)),
                   _fm_spec(HEADS * MOBA_DH, tm, nb)],
        out_shape=[jax.ShapeDtypeStruct((t // seq, nb, HEADS * HEAD_PAD, tm), BF16),
                   jax.ShapeDtypeStruct((t, HEADS * HEAD_PAD), BF16),
                   jax.ShapeDtypeStruct((t // seq, nb, HEADS * MOBA_DH, tm), BF16)],
        scratch_shapes=[pltpu.VMEM((nb, HEADS * MOBA_DH), F32)],
        compiler_params=_cp(("arbitrary",)),
        name="moba_prep",
    )(proj, proj, proj, lw["moba_qn_gain"], lw["moba_kn_gain"], lw["head_sum"], lw["head_expand"])


def _gdn_prep_kernel(gq_ref, gk_ref, gv_ref, misc_ref, wq_ref, wk_ref, wv_ref, alog_ref, dtb_ref, e_ref, et_ref,
                     q_out, k_out, v_out, gb_out, ext_scr, *, tiles_per_seq):
    tm = gq_ref.shape[0]
    pad = 8

    @pl.when(pl.program_id(0) % tiles_per_seq == 0)
    def _():
        ext_scr[:, 0:pad, :] = jnp.zeros((3, pad, ext_scr.shape[2]), F32)

    def conv_silu(s, x_ref, w_ref):
        ext_scr[s, pad:pad + tm, :] = x_ref[...]
        y = jnp.zeros(x_ref.shape, F32)
        for i in range(GDN_CONV):
            y = y + ext_scr[s, pl.ds(pad - (GDN_CONV - 1) + i, tm), :] * w_ref[i:i + 1, :]
        ext_scr[s, 0:pad, :] = ext_scr[s, tm:tm + pad, :]
        return _silu(y)

    def l2n(x):
        ss = _mm(x * x, e_ref[...], HI)
        return x * _mm(lax.rsqrt(ss + NORM_EPS), et_ref[...], HI)

    q_out[...] = l2n(conv_silu(0, gq_ref, wq_ref)) * (GDN_DK ** -0.5)
    k_out[...] = l2n(conv_silu(1, gk_ref, wk_ref))
    v_out[...] = conv_silu(2, gv_ref, wv_ref)
    misc = misc_ref[...]
    a = misc[:, MLA_ROPE:MLA_ROPE + HEADS]
    b = misc[:, MLA_ROPE + HEADS:MLA_ROPE + 2 * HEADS]
    g = -jnp.exp(alog_ref[...]) * jax.nn.softplus(a + dtb_ref[...])
    beta = jax.nn.sigmoid(b)
    gb_out[...] = jnp.concatenate([g, beta, jnp.zeros((tm, 128 - 2 * HEADS), F32)], axis=1)


def _gdn_prep(proj, lw, seq, tm=256):
    t = proj.shape[0]
    full = lambda shp: pl.BlockSpec(shp, lambda i: (0,) * len(shp))
    c0 = COL_GQ // 512
    tok = lambda w: pl.BlockSpec((tm, w), lambda i: (i, 0))
    return pl.pallas_call(
        functools.partial(_gdn_prep_kernel, tiles_per_seq=seq // tm),
        grid=(t // tm,),
        in_specs=[pl.BlockSpec((tm, 512), lambda i: (i, c0)),
                  pl.BlockSpec((tm, 512), lambda i: (i, c0 + 1)),
                  pl.BlockSpec((tm, 512), lambda i: (i, c0 + 2)),
                  pl.BlockSpec((tm, 128), lambda i: (i, 3)),
                  full((GDN_CONV, 512)), full((GDN_CONV, 512)), full((GDN_CONV, 512)),
                  full((1, HEADS)), full((1, HEADS)), full((512, HEADS)), full((HEADS, 512))],
        out_specs=[tok(512), tok(512), tok(512), tok(128)],
        out_shape=[jax.ShapeDtypeStruct((t, 512), F32)] * 3 + [jax.ShapeDtypeStruct((t, 128), F32)],
        scratch_shapes=[pltpu.VMEM((3, tm + 8, 512), F32)],
        compiler_params=_cp(("arbitrary",)),
        name="gdn_prep",
    )(proj, proj, proj, proj, lw["conv_wq"], lw["conv_wk"], lw["conv_wv"], lw["gdn_a_log"], lw["gdn_dt_bias"],
      lw["head_sum"], lw["head_expand"])


def _gdn_scan_kernel(q_ref, k_ref, v_ref, gb_ref, z_ref, og_ref, o_ref, s_scr, *, chunks):
    c = GDN_CHUNK

    @pl.when(pl.program_id(1) == 0)
    def _():
        s_scr[...] = jnp.zeros_like(s_scr)

    ri = lax.broadcasted_iota(jnp.int32, (c, c), 0)
    ci = lax.broadcasted_iota(jnp.int32, (c, c), 1)
    tri = ci <= ri
    strict = ci < ri
    ltri = jnp.where(tri, 1.0, 0.0)
    probs = [(cc, h) for cc in range(chunks) for h in range(HEADS)]
    rows = lambda cc: slice(cc * c, (cc + 1) * c)
    hsl = lambda h: slice(h * GDN_DK, (h + 1) * GDN_DK)
    gbs = [gb_ref[rows(cc), :] for cc in range(chunks)]
    gcs = [_mm(ltri, gb[:, 0:HEADS], HI) for gb in gbs]
    gcts = [g.T for g in gcs]
    q = [q_ref[rows(cc), hsl(h)] for cc, h in probs]
    k = [k_ref[rows(cc), hsl(h)] for cc, h in probs]
    v = [v_ref[rows(cc), hsl(h)] for cc, h in probs]
    gcol = [gcs[cc][:, h:h + 1] for cc, h in probs]
    grow = [gcts[cc][h:h + 1, :] for cc, h in probs]
    glast = [gcs[cc][c - 1:c, h:h + 1] for cc, h in probs]
    bcol = [gbs[cc][:, HEADS + h:HEADS + h + 1] for cc, h in probs]
    n = len(probs)
    egc = [jnp.exp(g) for g in gcol]
    decay = [jnp.where(tri, jnp.exp(jnp.where(tri, gcol[i] - grow[i], 0.0)), 0.0) for i in range(n)]
    kb = [k[i] * bcol[i] for i in range(n)]
    k16 = [a.astype(BF16) for a in k]
    kb16 = [a.astype(BF16) for a in kb]
    a_neg = [jnp.where(strict, -(_nt(kb16[i], k16[i]) * decay[i]), 0.0) for i in range(n)]
    x = [jnp.concatenate([v[i] * bcol[i], kb[i] * egc[i]], axis=1) for i in range(n)]
    def mm_split_rhs(a16_, rhs):
        hi = rhs.astype(BF16)
        lo = (rhs - hi.astype(F32)).astype(BF16)
        return _mm(a16_, hi) + _mm(a16_, lo)

    for lvl in range(6):
        a16 = [a.astype(BF16) for a in a_neg]
        if lvl < 5:
            r = [mm_split_rhs(a16[i], jnp.concatenate([a_neg[i], x[i]], axis=1)) for i in range(n)]
            a_neg = [ri_[:, 0:c] for ri_ in r]
            x = [x[i] + r[i][:, c:] for i in range(n)]
        else:
            x = [x[i] + mm_split_rhs(a16[i], x[i]) for i in range(n)]
    u = [xi[:, 0:GDN_DV] for xi in x]
    w16 = [xi[:, GDN_DV:].astype(BF16) for xi in x]
    attn16 = [(_nt(q[i].astype(BF16), k16[i]) * decay[i]).astype(BF16) for i in range(n)]
    qe16 = [(q[i] * egc[i]).astype(BF16) for i in range(n)]
    kd16 = [(k[i] * jnp.exp(glast[i] - gcol[i])).astype(BF16) for i in range(n)]
    eglast = [jnp.exp(g) for g in glast]
    state = [s_scr[h] for h in range(HEADS)]
    for cc in range(chunks):
        ids = [cc * HEADS + h for h in range(HEADS)]
        s16 = [s.astype(BF16) for s in state]
        v_new = [u[i] - _mm(w16[i], s16[h]) for h, i in enumerate(ids)]
        vn16 = [a.astype(BF16) for a in v_new]
        o = [_mm(qe16[i], s16[h]) + _mm(attn16[i], vn16[h]) for h, i in enumerate(ids)]
        state = [state[h] * eglast[i] + _tn(kd16[i], vn16[h]) for h, i in enumerate(ids)]
        o = [oh * lax.rsqrt(jnp.mean(oh * oh, axis=-1, keepdims=True) + NORM_EPS) * og_ref[...] for oh in o]
        o_ref[rows(cc), :] = jnp.concatenate(o, axis=1) * _silu(z_ref[rows(cc), :])
    for h in range(HEADS):
        s_scr[h] = state[h]


def _gdn_scan(qn, kn, vv, gb, proj, lw, bsz, seq, chunks=4):
    t = qn.shape[0]
    ct = chunks * GDN_CHUNK
    ns = seq // ct
    tok = lambda w: pl.BlockSpec((ct, w), lambda b, i: (b * ns + i, 0))
    return pl.pallas_call(
        functools.partial(_gdn_scan_kernel, chunks=chunks),
        grid=(bsz, ns),
        in_specs=[tok(512), tok(512), tok(512), tok(128),
                  pl.BlockSpec((ct, 512), lambda b, i: (b * ns + i, COL_GQ // 512 + 3)),
                  pl.BlockSpec((1, GDN_DV), lambda b, i: (0, 0))],
        out_specs=tok(512),
        out_shape=jax.ShapeDtypeStruct((t, 512), F32),
        scratch_shapes=[pltpu.VMEM((HEADS, GDN_DK, GDN_DV), F32)],
        compiler_params=_cp(("arbitrary", "arbitrary")),
        name="gdn_scan",
    )(qn, kn, vv, gb, proj, lw["gdn_o_gain"])


def _merge_kernel(oa_ref, ob_ref, oc_ref, g0_ref, g1_ref, g2_ref, x_ref, ga_ref, wb_ref, wo_ref, o_ref):
    y = None
    for o_n, g_n, n in ((oa_ref, g0_ref, 0), (ob_ref, g1_ref, 1), (oc_ref, g2_ref, 2)):
        term = jax.nn.sigmoid(g_n[...]) * _mm(o_n[...].astype(BF16), wb_ref[n])
        y = term if y is None else y + term
    o_ref[...] = x_ref[...] + ga_ref[0] * _mm(y.astype(BF16), wo_ref[...])


def _merge(o_mla, o_gdn, o_moba, proj, x, g_a, lw, seq, tm=512):
    t, d = x.shape
    bsz = g_a.shape[0]
    tok = lambda w: pl.BlockSpec((tm, w), lambda i: (i, 0))
    gcol = COL_GATE // d
    return pl.pallas_call(
        _merge_kernel,
        grid=(t // tm,),
        in_specs=[tok(BRANCH_W), tok(BRANCH_W), tok(BRANCH_W),
                  pl.BlockSpec((tm, d), lambda i: (i, gcol)),
                  pl.BlockSpec((tm, d), lambda i: (i, gcol + 1)),
                  pl.BlockSpec((tm, d), lambda i: (i, gcol + 2)),
                  tok(d),
                  pl.BlockSpec((1, 1, d), lambda i: ((i * tm) // seq, 0, 0)),
                  pl.BlockSpec((N_BRANCH, BRANCH_W, d), lambda i: (0, 0, 0)),
                  pl.BlockSpec((d, d), lambda i: (0, 0))],
        out_specs=tok(d),
        out_shape=jax.ShapeDtypeStruct((t, d), F32),
        compiler_params=_cp(("arbitrary",)),
        name="branch_merge",
    )(o_mla, o_gdn, o_moba, proj, proj, proj, x, g_a.reshape(bsz, 1, d), lw["w_branch"], lw["w_out"])


def _topk_rows(s, k):
    n = s.shape[0]
    ri = lax.broadcasted_iota(jnp.int32, s.shape, 0).astype(F32)
    vals, idxs = [], []
    for _ in range(k):
        mx = jnp.max(s, axis=0, keepdims=True)
        idx = jnp.min(jnp.where(s == mx, ri, float(n)), axis=0, keepdims=True)
        vals.append(mx)
        idxs.append(idx)
        s = jnp.where(ri == idx, -jnp.inf, s)
    return jnp.concatenate(vals, axis=0), jnp.concatenate(idxs, axis=0).astype(jnp.int32)


def _peer_select_kernel(qry_ref, keys_ref, ids_out, gates_out):
    kk = PEER_TOPK
    ids_rows, gate_rows = [], []
    for h in range(PEER_HEADS):
        halves = []
        for p in range(2):
            g = h * 2 + p
            s_t = _nt(keys_ref[g], qry_ref[:, g * 128:(g + 1) * 128], HI)
            halves.append(_topk_rows(s_t, kk))
        (v1, i1), (v2, i2) = halves
        cand, a_of, b_of = [], [], []
        for a in range(kk):
            nb = kk // (a + 1)
            cand.append(v1[a:a + 1, :] + v2[0:nb, :])
            a_of += [a] * nb
            b_of += list(range(nb))
        cand = jnp.concatenate(cand, axis=0)
        top_s, top_r = _topk_rows(cand, kk)
        a_sel = jnp.zeros_like(top_r)
        b_sel = jnp.zeros_like(top_r)
        for r, (a, b) in enumerate(zip(a_of, b_of)):
            hit = top_r == r
            a_sel = jnp.where(hit, a, a_sel)
            b_sel = jnp.where(hit, b, b_sel)
        e1 = jnp.zeros_like(top_r)
        e2 = jnp.zeros_like(top_r)
        for a in range(kk):
            e1 = jnp.where(a_sel == a, i1[a:a + 1, :], e1)
            e2 = jnp.where(b_sel == a, i2[a:a + 1, :], e2)
        ids_rows.append(e1 * PEER_NKEYS + e2)
        ex = jnp.exp(top_s - top_s[0:1, :])
        gate_rows.append(ex / jnp.sum(ex, axis=0, keepdims=True))
    ids_out[...] = jnp.concatenate(ids_rows, axis=0).T
    gates_out[...] = jnp.concatenate(gate_rows, axis=0).T


def _peer_select(qry, sub_keys, tm=256):
    t = qry.shape[0]
    return pl.pallas_call(
        _peer_select_kernel,
        grid=(t // tm,),
        in_specs=[pl.BlockSpec((tm, 2 * PEER_HEADS * 128), lambda i: (i, 0)),
                  pl.BlockSpec((2 * PEER_HEADS, PEER_NKEYS, 128), lambda i: (0, 0, 0))],
        out_specs=[pl.BlockSpec((tm, PEER_SEL), lambda i: (i, 0)),
                   pl.BlockSpec((tm, PEER_SEL), lambda i: (i, 0))],
        out_shape=[jax.ShapeDtypeStruct((t, PEER_SEL), jnp.int32), jax.ShapeDtypeStruct((t, PEER_SEL), F32)],
        compiler_params=_cp(("arbitrary",)),
        name="peer_select",
    )(qry, sub_keys)


def _peer_apply_kernel(ids_ref, gates_ref, h_ref, x_ref, gf_ref, uv_hbm, o_ref, buf0, buf1, sems, *, tt):
    s = pl.program_id(0)
    ns = pl.num_programs(0) - 1
    slot = s % 2
    buf = (buf0, buf1)

    def start_rows(dst_slot):
        for t in range(tt):
            for e in range(PEER_SEL):
                pltpu.make_async_copy(uv_hbm.at[ids_ref[t, e]], buf[dst_slot].at[t, e],
                                      sems.at[dst_slot]).start(priority=e % 2)

    def wait_rows(src_slot):
        pltpu.make_async_copy(buf[1 - src_slot], buf[src_slot], sems.at[src_slot]).wait()

    def reduce_tile(src_slot):
        src = buf[src_slot]
        nsub = EXPERT_ROWS // 2
        prod = src[:, :, 0:nsub, :].astype(F32) * h_ref[...][:, None, :, :]
        act = jnp.sum(jnp.sum(prod, axis=2), axis=-1)
        gel = 0.5 * act * (1.0 + lax.erf(act * (2.0 ** -0.5)))
        wgt = (gates_ref[...] * gel)[:, :, None, None]
        out = jnp.sum(wgt * src[:, :, nsub:, :].astype(F32), axis=1)
        o_ref[...] = x_ref[...] + gf_ref[...] * out

    def steady(parity):
        wait_rows(1 - parity)
        start_rows(parity)
        reduce_tile(1 - parity)

    def last(parity):
        wait_rows(1 - parity)
        reduce_tile(1 - parity)

    for parity in range(2):
        pl.when((s >= 1) & (s < ns) & (slot == parity))(functools.partial(steady, parity))
    pl.when(s == 0)(functools.partial(start_rows, 0))
    for parity in range(2):
        pl.when((s == ns) & (slot == parity))(functools.partial(last, parity))


def _peer_apply(ids, gates, h2, x, g_f, uv, seq, tt=16):
    t, d = x.shape
    bsz = g_f.shape[0]
    ns = t // tt
    sub = d // 128
    cur = lambda i: jnp.maximum(i - 1, 0)
    tile3 = pl.BlockSpec((tt, sub, 128), lambda i: (cur(i), 0, 0))
    out = pl.pallas_call(
        functools.partial(_peer_apply_kernel, tt=tt),
        grid=(ns + 1,),
        in_specs=[pl.BlockSpec((tt, PEER_SEL), lambda i: (jnp.minimum(i, ns - 1), 0), memory_space=pltpu.SMEM),
                  pl.BlockSpec((tt, PEER_SEL), lambda i: (cur(i), 0)), tile3, tile3,
                  pl.BlockSpec((1, sub, 128), lambda i: ((cur(i) * tt) // seq, 0, 0)),
                  pl.BlockSpec(memory_space=pl.ANY)],
        out_specs=tile3,
        out_shape=jax.ShapeDtypeStruct((t, sub, 128), F32),
        scratch_shapes=[pltpu.VMEM((tt, PEER_SEL, EXPERT_ROWS, 128), uv.dtype),
                        pltpu.VMEM((tt, PEER_SEL, EXPERT_ROWS, 128), uv.dtype),
                        pltpu.SemaphoreType.DMA((2,))],
        compiler_params=_cp(("arbitrary",)),
        name="peer_apply",
    )(ids, gates, h2.reshape(t, sub, 128), x.reshape(t, sub, 128), g_f.reshape(bsz, sub, 128), uv)
    return out.reshape(t, d)


def _prep_layer_weights(l, w_in, mla_q_gain, mla_w_uq, mla_kv_gain, mla_w_ukv, mla_qn_gain, mla_kn_gain,
                        gdn_conv_w, gdn_a_log, gdn_dt_bias, gdn_o_gain, moba_qn_gain, moba_kn_gain,
                        w_branch, w_out, peer_w_query, peer_sub_keys):
    d = D_MODEL
    w = w_in[l]
    o = 0
    parts = {}
    for name, width in (("cq", 256), ("ckv", 128), ("kr", 32), ("gq", 512), ("gk", 512), ("gv", 512), ("gz", 512),
                        ("ga", 8), ("gb", 8), ("mq", 512), ("mk", 512), ("mv", 512), ("gate", 3072)):
        parts[name] = w[:, o:o + width]
        o += width
    w_in_p = jnp.concatenate(
        [parts["cq"], parts["ckv"], parts["kr"], parts["ga"], parts["gb"], jnp.zeros((d, 80), F32),
         parts["gq"], parts["gk"], parts["gv"], parts["gz"], parts["mv"], parts["mq"], parts["mk"], parts["gate"]],
        axis=1)
    w_in_hi = w_in_p.astype(BF16)
    sel = slice(COL_MQ, COL_MQ + 1024)
    w_in_lo = (w_in_p[:, sel] - w_in_hi[:, sel].astype(F32)).astype(BF16)
    wq_hi = peer_w_query[l].astype(BF16)
    wq_lo = (peer_w_query[l] - wq_hi.astype(F32)).astype(BF16)

    def pad_heads(m, width):
        r = m.shape[0]
        return jnp.pad(m.reshape(r, HEADS, width), ((0, 0), (0, 0), (0, HEAD_PAD - width))).reshape(r, HEADS * HEAD_PAD)

    ukv = mla_w_ukv[l].reshape(MLA_KV_RANK, HEADS, MLA_NOPE + MLA_V)
    pad1 = lambda g, n: jnp.pad(g, (0, n - g.shape[0])).reshape(1, n)
    lane = jnp.arange(HEAD_PAD)
    half = MLA_ROPE // 2
    inv_freq = jnp.where((lane >= MLA_NOPE) & (lane < MLA_QK),
                         ROPE_THETA ** (-((lane - MLA_NOPE) % half).astype(F32) / half), 0.0).reshape(1, HEAD_PAD)
    head_sum = jnp.repeat(jnp.eye(HEADS, dtype=F32), 64, axis=0)
    return dict(
        w_in=w_in_hi, w_in_lo=w_in_lo,
        mla_q_gain=mla_q_gain[l].reshape(1, -1),
        mla_w_uq=pad_heads(mla_w_uq[l], MLA_QK).astype(BF16),
        mla_kv_gain=mla_kv_gain[l].reshape(1, -1),
        mla_w_uk=pad_heads(ukv[:, :, :MLA_NOPE].reshape(MLA_KV_RANK, -1), MLA_NOPE).astype(BF16),
        mla_w_uv=ukv[:, :, MLA_NOPE:].reshape(MLA_KV_RANK, -1).astype(BF16),
        mla_qn_gain=pad1(mla_qn_gain[l], HEAD_PAD),
        mla_kn_gain=pad1(mla_kn_gain[l], HEAD_PAD),
        rope_inv_freq=inv_freq,
        conv_wq=gdn_conv_w[l][:, 0:512], conv_wk=gdn_conv_w[l][:, 512:1024], conv_wv=gdn_conv_w[l][:, 1024:1536],
        gdn_a_log=gdn_a_log[l].reshape(1, -1), gdn_dt_bias=gdn_dt_bias[l].reshape(1, -1),
        gdn_o_gain=gdn_o_gain[l].reshape(1, -1),
        moba_qn_gain=jnp.tile(moba_qn_gain[l], HEADS).reshape(1, -1),
        moba_kn_gain=jnp.tile(moba_kn_gain[l], HEADS).reshape(1, -1),
        head_sum=head_sum, head_expand=head_sum.T,
        w_branch=w_branch[l].astype(BF16), w_out=w_out[l].astype(BF16),
        peer_w_query=wq_hi, peer_w_query_lo=wq_lo,
        peer_sub_keys=peer_sub_keys[l].reshape(2 * PEER_HEADS, PEER_NKEYS, -1),
    )


def kernel(x, c, positions, w_mod, b_mod, w_in, mla_q_gain, mla_w_uq, mla_kv_gain, mla_w_ukv, mla_qn_gain, mla_kn_gain, gdn_conv_w, gdn_a_log, gdn_dt_bias, gdn_o_gain, moba_qn_gain, moba_kn_gain, w_branch, w_out, peer_w_query, peer_sub_keys, peer_u, peer_v):
    bsz, seq, d = x.shape
    t = bsz * seq
    depth = w_mod.shape[0]
    xt = x.reshape(t, d)
    pos = positions.reshape(t, 1)
    mod = _mod_all(c, w_mod, b_mod)
    for l in range(depth):
        lw = _prep_layer_weights(l, w_in, mla_q_gain, mla_w_uq, mla_kv_gain, mla_w_ukv, mla_qn_gain, mla_kn_gain,
                                 gdn_conv_w, gdn_a_log, gdn_dt_bias, gdn_o_gain, moba_qn_gain, moba_kn_gain,
                                 w_branch, w_out, peer_w_query, peer_sub_keys)
        sh_a, sc_a, g_a, sh_f, sc_f, g_f = [mod[l, :, i * d:(i + 1) * d] for i in range(6)]
        proj, _ = _modulate_matmul(xt, sh_a, sc_a, lw["w_in"], lw["w_in_lo"], COL_MQ // 1024, seq)
        q, k, v = _mla_prep(proj, pos, lw, seq)
        o_mla = _causal_attention(q, k, v, bsz, seq)
        q, k, v = _moba_prep(proj, lw, seq)
        o_moba = _causal_attention(q, k, v, bsz, seq)
        qn, kn, vv, gb = _gdn_prep(proj, lw, seq)
        o_gdn = _gdn_scan(qn, kn, vv, gb, proj, lw, bsz, seq)
        xt = _merge(o_mla, o_gdn, o_moba, proj, xt, g_a, lw, seq)
        qry, h2 = _modulate_matmul(xt, sh_f, sc_f, lw["peer_w_query"], lw["peer_w_query_lo"], 0, seq)
        ids, gates = _peer_select(qry, lw["peer_sub_keys"])
        uv = jnp.concatenate([peer_u[l], peer_v[l]], axis=1).reshape(-1, EXPERT_ROWS, 128).astype(BF16)
        xt = _peer_apply(ids, gates, h2, xt, g_f, uv, seq)
    return xt.reshape(bsz, seq, d)
```

```python
import functools

import jax
import jax.numpy as jnp
from jax import lax
from jax.experimental import pallas as pl
from jax.experimental.pallas import tpu as pltpu

F32 = jnp.float32
BF16 = jnp.bfloat16
HI = lax.Precision.HIGHEST

D_MODEL = 1024
MLA_HEADS = 8
MLA_Q_RANK = 256
MLA_KV_RANK = 128
MLA_NOPE = 64
MLA_ROPE = 32
MLA_V = 64
MLA_QK = MLA_NOPE + MLA_ROPE
ROPE_THETA = 10000.0
HEADS = 8
HEAD_PAD = 128
GDN_DK = 64
GDN_DV = 64
GDN_CONV = 4
GDN_CHUNK = 64
MOBA_DH = 64
MOBA_BLOCK = 256
ATTN_BLOCK = MOBA_BLOCK
MOBA_TOPK = 3
N_BRANCH = 3
BRANCH_W = 512
PEER_HEADS = 8
PEER_NKEYS = 128
PEER_TOPK = 16
PEER_SEL = PEER_HEADS * PEER_TOPK
EXPERT_ROWS = 2 * D_MODEL // 128
NORM_EPS = 1e-6
NEG_INF = -1e30

COL_LAT = 0
COL_GQ = 512
COL_MV = 2560
COL_MQ = 3072
COL_GATE = 4096
IN_COLS_PAD = 7168

VMEM_LIMIT = 56 * 1024 * 1024


def _cp(sem, vmem=None):
    return pltpu.CompilerParams(dimension_semantics=sem, vmem_limit_bytes=vmem or VMEM_LIMIT)


def _nt(a, b, precision=None):
    return lax.dot_general(a, b, (((1,), (1,)), ((), ())), precision=precision, preferred_element_type=F32)


def _tn(a, b, precision=None):
    return lax.dot_general(a, b, (((0,), (0,)), ((), ())), precision=precision, preferred_element_type=F32)


def _mm(a, b, precision=None):
    return jnp.dot(a, b, precision=precision, preferred_element_type=F32)


def _silu(x):
    return x * jax.nn.sigmoid(x)


def _mod_kernel(c_ref, w_ref, b_ref, o_ref):
    c = c_ref[...]
    o_ref[0] = _mm(_silu(c), w_ref[0], HI) + b_ref[0]


def _mod_all(c, w_mod, b_mod):
    depth, d, n = w_mod.shape
    bsz = c.shape[0]
    tn = 1024
    return pl.pallas_call(
        _mod_kernel,
        grid=(depth, n // tn),
        in_specs=[pl.BlockSpec((bsz, d), lambda l, j: (0, 0)),
                  pl.BlockSpec((1, d, tn), lambda l, j: (l, 0, j)),
                  pl.BlockSpec((1, 1, tn), lambda l, j: (l, 0, j))],
        out_specs=pl.BlockSpec((1, bsz, tn), lambda l, j: (l, 0, j)),
        out_shape=jax.ShapeDtypeStruct((depth, bsz, n), F32),
        compiler_params=_cp(("arbitrary", "arbitrary")),
        name="adaln_mod",
    )(c, w_mod, b_mod.reshape(depth, 1, n))


def _modmm_kernel(x_ref, sh_ref, sc_ref, w_ref, wlo_ref, o_ref, h_ref, hb_scr, hlo_scr, *, lo_first, lo_tiles):
    j = pl.program_id(1)

    @pl.when(j == 0)
    def _():
        x = x_ref[...]
        h = x * lax.rsqrt(jnp.mean(x * x, axis=-1, keepdims=True) + NORM_EPS)
        h = h * (1.0 + sc_ref[0]) + sh_ref[0]
        h_ref[...] = h
        hb = h.astype(BF16)
        hb_scr[...] = hb
        hlo_scr[...] = (h - hb.astype(F32)).astype(BF16)

    precise = (j >= lo_first) & (j < lo_first + lo_tiles)

    @pl.when(precise)
    def _():
        o_ref[...] = (_mm(hb_scr[...], w_ref[...]) + _mm(hb_scr[...], wlo_ref[...])
                      + _mm(hlo_scr[...], w_ref[...]))

    @pl.when(jnp.logical_not(precise))
    def _():
        o_ref[...] = _mm(hb_scr[...], w_ref[...])


def _modulate_matmul(x, shift, scale, w_bf16, w_lo, lo_first, seq, tm=1024, tn=1024):
    t, d = x.shape
    n = w_bf16.shape[1]
    lo_tiles = w_lo.shape[1] // tn
    bsz = shift.shape[0]
    bidx = lambda i, j: ((i * tm) // seq, 0, 0)
    return pl.pallas_call(
        functools.partial(_modmm_kernel, lo_first=lo_first, lo_tiles=lo_tiles),
        grid=(t // tm, n // tn),
        in_specs=[pl.BlockSpec((tm, d), lambda i, j: (i, 0)),
                  pl.BlockSpec((1, 1, d), bidx),
                  pl.BlockSpec((1, 1, d), bidx),
                  pl.BlockSpec((d, tn), lambda i, j: (0, j)),
                  pl.BlockSpec((d, tn), lambda i, j: (0, jnp.clip(j - lo_first, 0, lo_tiles - 1)))],
        out_specs=[pl.BlockSpec((tm, tn), lambda i, j: (i, j)),
                   pl.BlockSpec((tm, d), lambda i, j: (i, 0))],
        out_shape=[jax.ShapeDtypeStruct((t, n), F32), jax.ShapeDtypeStruct((t, d), F32)],
        scratch_shapes=[pltpu.VMEM((tm, d), BF16), pltpu.VMEM((tm, d), BF16)],
        compiler_params=_cp(("arbitrary", "arbitrary")),
        name="modulate_matmul",
    )(x, shift.reshape(bsz, 1, d), scale.reshape(bsz, 1, d), w_bf16, w_lo)


def _mla_prep_kernel(p_ref, pos_ref, qg_ref, wuq_ref, kvg_ref, wuk_ref, wuv_ref, qng_ref, kng_ref, invf_ref,
                     q_out, k_out, v_out):
    tm = p_ref.shape[0]
    cq = p_ref[:, 0:MLA_Q_RANK]
    ckv = p_ref[:, MLA_Q_RANK:MLA_Q_RANK + MLA_KV_RANK]
    misc = p_ref[:, MLA_Q_RANK + MLA_KV_RANK:MLA_Q_RANK + MLA_KV_RANK + 128]

    def rms(v, n):
        return v * lax.rsqrt(jnp.sum(v * v, axis=-1, keepdims=True) * (1.0 / n) + NORM_EPS)

    qn = (rms(cq, MLA_Q_RANK) * qg_ref[...]).astype(BF16)
    q_all = _mm(qn, wuq_ref[...])
    kvn = (rms(ckv, MLA_KV_RANK) * kvg_ref[...]).astype(BF16)
    k_all = _mm(kvn, wuk_ref[...])
    v_out[0, 0] = _mm(kvn, wuv_ref[...]).T.astype(BF16)

    lane = lax.broadcasted_iota(jnp.int32, (tm, HEAD_PAD), 1)
    in_rope = (lane >= MLA_NOPE) & (lane < MLA_QK)
    k_rope = jnp.where(in_rope, pltpu.roll(misc, MLA_NOPE, 1), 0.0)
    ang = pos_ref[...].astype(F32) * invf_ref[...]
    cos = jnp.cos(ang)
    sin = jnp.sin(ang)
    half = MLA_ROPE // 2
    c_tab = jnp.where(lane < MLA_NOPE, 1.0, jnp.where(in_rope, cos, 0.0))
    s_lo = jnp.where(in_rope & (lane < MLA_NOPE + half), -sin, 0.0)
    s_hi = jnp.where(in_rope & (lane >= MLA_NOPE + half), sin, 0.0)

    def finish(xh, gain):
        xh = xh * lax.rsqrt(jnp.sum(xh * xh, axis=-1, keepdims=True) * (1.0 / MLA_QK) + NORM_EPS) * gain
        return xh * c_tab + pltpu.roll(xh, HEAD_PAD - half, 1) * s_lo + pltpu.roll(xh, half, 1) * s_hi

    scale = MLA_QK ** -0.5
    for h in range(MLA_HEADS):
        sl = slice(h * HEAD_PAD, (h + 1) * HEAD_PAD)
        q_out[0, 0, sl, :] = (finish(q_all[:, sl], qng_ref[...]) * scale).T.astype(BF16)
        k_out[:, sl] = finish(k_all[:, sl] + k_rope, kng_ref[...]).astype(BF16)


def _fm_spec(rows, tm, tiles_per_seq):
    return pl.BlockSpec((1, 1, rows, tm), lambda i: (i // tiles_per_seq, i % tiles_per_seq, 0, 0))


def _mla_prep(proj, pos, lw, seq):
    t = proj.shape[0]
    tm = ATTN_BLOCK
    nblk = seq // tm
    full = lambda shp: pl.BlockSpec(shp, lambda i: (0,) * len(shp))
    return pl.pallas_call(
        _mla_prep_kernel,
        grid=(t // tm,),
        in_specs=[pl.BlockSpec((tm, 512), lambda i: (i, 0)),
                  pl.BlockSpec((tm, 1), lambda i: (i, 0)),
                  full((1, MLA_Q_RANK)), full((MLA_Q_RANK, HEADS * HEAD_PAD)),
                  full((1, MLA_KV_RANK)), full((MLA_KV_RANK, HEADS * HEAD_PAD)),
                  full((MLA_KV_RANK, HEADS * MLA_V)),
                  full((1, HEAD_PAD)), full((1, HEAD_PAD)), full((1, HEAD_PAD))],
        out_specs=[_fm_spec(HEADS * HEAD_PAD, tm, nblk),
                   pl.BlockSpec((tm, HEADS * HEAD_PAD), lambda i: (i, 0)),
                   _fm_spec(HEADS * MLA_V, tm, nblk)],
        out_shape=[jax.ShapeDtypeStruct((t // seq, nblk, HEADS * HEAD_PAD, tm), BF16),
                   jax.ShapeDtypeStruct((t, HEADS * HEAD_PAD), BF16),
                   jax.ShapeDtypeStruct((t // seq, nblk, HEADS * MLA_V, tm), BF16)],
        compiler_params=_cp(("arbitrary",)),
        name="mla_prep",
    )(proj, pos, lw["mla_q_gain"], lw["mla_w_uq"], lw["mla_kv_gain"], lw["mla_w_uk"], lw["mla_w_uv"],
      lw["mla_qn_gain"], lw["mla_kn_gain"], lw["rope_inv_freq"])


def _attn_kernel(qt_ref, k_ref, vt_ref, o_ref, *, blk, hg):
    i = pl.program_id(2)
    half = blk // 2
    kidx = lax.broadcasted_iota(jnp.int32, (half, blk), 0)
    qidx = lax.broadcasted_iota(jnp.int32, (half, blk), 1)
    heads = range(hg)
    qts = [qt_ref[0, 0, hh * HEAD_PAD:(hh + 1) * HEAD_PAD, :] for hh in heads]

    chains = [(hh, kh) for hh in heads for kh in range(2)]
    n = len(chains)

    def scores(j):
        start = pl.multiple_of(j * blk, blk)
        rows = [pl.ds(pl.multiple_of(start + kh * half, half), half) for kh in range(2)]
        return [_mm(k_ref[rows[kh], hh * HEAD_PAD:(hh + 1) * HEAD_PAD], qts[hh]) for hh, kh in chains]

    def update(j, st, carry, masked):
        ms, ls, accs = carry
        if masked:
            st = [jnp.where(kidx + kh * half <= qidx, s, NEG_INF) for s, (hh, kh) in zip(st, chains)]
        m_new = [jnp.maximum(ms[c], jnp.max(st[c], axis=0, keepdims=True)) for c in range(n)]
        pt = [jnp.exp(st[c] - m_new[c]) for c in range(n)]
        alpha = [jnp.exp(ms[c] - m_new[c]) for c in range(n)]
        ls = [alpha[c] * ls[c] + jnp.sum(pt[c], axis=0, keepdims=True) for c in range(n)]
        accs = [alpha[c] * accs[c]
                + _mm(vt_ref[0, j, hh * MLA_V:(hh + 1) * MLA_V, kh * half:(kh + 1) * half], pt[c].astype(BF16))
                for c, (hh, kh) in enumerate(chains)]
        return tuple(m_new), tuple(ls), tuple(accs)

    def pair(p, carry):
        j = 2 * p
        st0 = scores(j)
        st1 = scores(j + 1)
        carry = update(j, st0, carry, False)
        return update(j + 1, st1, carry, False)

    def finish(carry):
        ms, ls, accs = carry
        outs = []
        for hh in heads:
            a, b = 2 * hh, 2 * hh + 1
            m = jnp.maximum(ms[a], ms[b])
            wa, wb = jnp.exp(ms[a] - m), jnp.exp(ms[b] - m)
            outs.append((wa * accs[a] + wb * accs[b]) / (wa * ls[a] + wb * ls[b]))
        o_ref[...] = jnp.concatenate(outs, axis=0).T

    init = (tuple(jnp.full((1, blk), -jnp.inf, F32) for _ in chains),
            tuple(jnp.zeros((1, blk), F32) for _ in chains),
            tuple(jnp.zeros((MLA_V, blk), F32) for _ in chains))
    carry = lax.fori_loop(0, i // 2, pair, init)

    @pl.when(i % 2 == 0)
    def _():
        finish(update(i, scores(i), carry, True))

    @pl.when(i % 2 == 1)
    def _():
        st0 = scores(i - 1)
        st1 = scores(i)
        finish(update(i, st1, update(i - 1, st0, carry, False), True))


def _causal_attention(qt, k, vt, bsz, seq, hg=4):
    t = k.shape[0]
    blk = ATTN_BLOCK
    nq = seq // blk
    return pl.pallas_call(
        functools.partial(_attn_kernel, blk=blk, hg=hg),
        grid=(bsz, HEADS // hg, nq),
        in_specs=[pl.BlockSpec((1, 1, hg * HEAD_PAD, blk), lambda b, g, i: (b, i, g, 0)),
                  pl.BlockSpec((seq, hg * HEAD_PAD), lambda b, g, i: (b, g)),
                  pl.BlockSpec((1, nq, hg * MLA_V, blk), lambda b, g, i: (b, 0, g, 0))],
        out_specs=pl.BlockSpec((blk, hg * MLA_V), lambda b, g, i: (b * nq + i, g)),
        out_shape=jax.ShapeDtypeStruct((t, HEADS * MLA_V), F32),
        compiler_params=_cp(("arbitrary", "arbitrary", "arbitrary")),
        name="causal_attention",
    )(qt, k, vt)


def _moba_prep_kernel(mq_ref, mk_ref, mv_ref, qg_ref, kg_ref, e_ref, et_ref, q_out, k_out, v_out, kmean_scr, *, nb):
    tm = mq_ref.shape[0]
    n = pl.program_id(0) % nb

    @pl.when(n == 0)
    def _():
        kmean_scr[...] = jnp.zeros_like(kmean_scr)

    def headnorm(x, gain):
        ss = _mm(x * x, e_ref[...], HI)
        inv = lax.rsqrt(ss * (1.0 / MOBA_DH) + NORM_EPS)
        return x * _mm(inv, et_ref[...], HI) * gain

    qn = headnorm(mq_ref[...], qg_ref[...])
    kn = headnorm(mk_ref[...], kg_ref[...])
    v_out[0, 0] = mv_ref[...].T.astype(BF16)
    kmean_scr[pl.ds(n, 1), :] = jnp.mean(kn, axis=0, keepdims=True)
    km = kmean_scr[...]

    lane = lax.broadcasted_iota(jnp.int32, (tm, nb), 1)
    zpad = jnp.zeros((tm, HEAD_PAD - MOBA_DH - nb), F32)
    onehot = jnp.where(lane == n, 1.0, 0.0)
    row = lax.broadcasted_iota(jnp.int32, (nb, tm), 0)
    zpad_t = jnp.zeros((HEAD_PAD - MOBA_DH - nb, tm), F32)
    qn_t = (qn * (MOBA_DH ** -0.5)).T
    for h in range(HEADS):
        sl = slice(h * MOBA_DH, (h + 1) * MOBA_DH)
        gate = _nt(km[:, sl], qn[:, sl], HI)
        gate = jnp.where(row < n, gate, -jnp.inf)
        pen = jnp.full((nb, tm), NEG_INF, F32)
        for r in range(MOBA_TOPK):
            mx = jnp.max(gate, axis=0, keepdims=True)
            idx = jnp.min(jnp.where(gate == mx, row, nb), axis=0, keepdims=True)
            hit = (row == idx) & (r < n)
            pen = jnp.where(hit, 0.0, pen)
            gate = jnp.where(row == idx, -jnp.inf, gate)
        pen = jnp.where(row == n, 0.0, pen)
        osl = slice(h * HEAD_PAD, (h + 1) * HEAD_PAD)
        q_out[0, 0, osl, :] = jnp.concatenate([qn_t[sl, :], pen, zpad_t], axis=0).astype(BF16)
        k_out[:, osl] = jnp.concatenate([kn[:, sl], onehot, zpad], axis=1).astype(BF16)


def _moba_prep(proj, lw, seq):
    t = proj.shape[0]
    tm = MOBA_BLOCK
    nb = seq // tm
    full = lambda shp: pl.BlockSpec(shp, lambda i: (0,) * len(shp))
    c0 = COL_MQ // 512
    return pl.pallas_call(
        functools.partial(_moba_prep_kernel, nb=nb),
        grid=(t // tm,),
        in_specs=[pl.BlockSpec((tm, 512), lambda i: (i, c0)),
                  pl.BlockSpec((tm, 512), lambda i: (i, c0 + 1)),
                  pl.BlockSpec((tm, 512), lambda i: (i, COL_MV // 512)),
                  full((1, 512)), full((1, 512)), full((512, HEADS)), full((HEADS, 512))],
        out_specs=[_fm_spec(HEADS * HEAD_PAD, tm, nb),
                   pl.BlockSpec((tm, HEADS * HEAD_PAD), lambda i: (i, 0)),
                   _fm_spec(HEADS * MOBA_DH, tm, nb)],
        out_shape=[jax.ShapeDtypeStruct((t // seq, nb, HEADS * HEAD_PAD, tm), BF16),
                   jax.ShapeDtypeStruct((t, HEADS * HEAD_PAD), BF16),
                   jax.ShapeDtypeStruct((t // seq, nb, HEADS * MOBA_DH, tm), BF16)],
        scratch_shapes=[pltpu.VMEM((nb, HEADS * MOBA_DH), F32)],
        compiler_params=_cp(("arbitrary",)),
        name="moba_prep",
    )(proj, proj, proj, lw["moba_qn_gain"], lw["moba_kn_gain"], lw["head_sum"], lw["head_expand"])


def _gdn_prep_kernel(gq_ref, gk_ref, gv_ref, misc_ref, wq_ref, wk_ref, wv_ref, alog_ref, dtb_ref, e_ref, et_ref,
                     q_out, k_out, v_out, gb_out, ext_scr, *, tiles_per_seq):
    tm = gq_ref.shape[0]
    pad = 8

    @pl.when(pl.program_id(0) % tiles_per_seq == 0)
    def _():
        ext_scr[:, 0:pad, :] = jnp.zeros((3, pad, ext_scr.shape[2]), F32)

    def conv_silu(s, x_ref, w_ref):
        ext_scr[s, pad:pad + tm, :] = x_ref[...]
        y = jnp.zeros(x_ref.shape, F32)
        for i in range(GDN_CONV):
            y = y + ext_scr[s, pl.ds(pad - (GDN_CONV - 1) + i, tm), :] * w_ref[i:i + 1, :]
        ext_scr[s, 0:pad, :] = ext_scr[s, tm:tm + pad, :]
        return _silu(y)

    def l2n(x):
        ss = _mm(x * x, e_ref[...], HI)
        return x * _mm(lax.rsqrt(ss + NORM_EPS), et_ref[...], HI)

    q_out[...] = l2n(conv_silu(0, gq_ref, wq_ref)) * (GDN_DK ** -0.5)
    k_out[...] = l2n(conv_silu(1, gk_ref, wk_ref))
    v_out[...] = conv_silu(2, gv_ref, wv_ref)
    misc = misc_ref[...]
    a = misc[:, MLA_ROPE:MLA_ROPE + HEADS]
    b = misc[:, MLA_ROPE + HEADS:MLA_ROPE + 2 * HEADS]
    g = -jnp.exp(alog_ref[...]) * jax.nn.softplus(a + dtb_ref[...])
    beta = jax.nn.sigmoid(b)
    gb_out[...] = jnp.concatenate([g, beta, jnp.zeros((tm, 128 - 2 * HEADS), F32)], axis=1)


def _gdn_prep(proj, lw, seq, tm=256):
    t = proj.shape[0]
    full = lambda shp: pl.BlockSpec(shp, lambda i: (0,) * len(shp))
    c0 = COL_GQ // 512
    tok = lambda w: pl.BlockSpec((tm, w), lambda i: (i, 0))
    return pl.pallas_call(
        functools.partial(_gdn_prep_kernel, tiles_per_seq=seq // tm),
        grid=(t // tm,),
        in_specs=[pl.BlockSpec((tm, 512), lambda i: (i, c0)),
                  pl.BlockSpec((tm, 512), lambda i: (i, c0 + 1)),
                  pl.BlockSpec((tm, 512), lambda i: (i, c0 + 2)),
                  pl.BlockSpec((tm, 128), lambda i: (i, 3)),
                  full((GDN_CONV, 512)), full((GDN_CONV, 512)), full((GDN_CONV, 512)),
                  full((1, HEADS)), full((1, HEADS)), full((512, HEADS)), full((HEADS, 512))],
        out_specs=[tok(512), tok(512), tok(512), tok(128)],
        out_shape=[jax.ShapeDtypeStruct((t, 512), F32)] * 3 + [jax.ShapeDtypeStruct((t, 128), F32)],
        scratch_shapes=[pltpu.VMEM((3, tm + 8, 512), F32)],
        compiler_params=_cp(("arbitrary",)),
        name="gdn_prep",
    )(proj, proj, proj, proj, lw["conv_wq"], lw["conv_wk"], lw["conv_wv"], lw["gdn_a_log"], lw["gdn_dt_bias"],
      lw["head_sum"], lw["head_expand"])


def _gdn_scan_kernel(q_ref, k_ref, v_ref, gb_ref, z_ref, og_ref, o_ref, s_scr, *, chunks):
    c = GDN_CHUNK

    @pl.when(pl.program_id(1) == 0)
    def _():
        s_scr[...] = jnp.zeros_like(s_scr)

    ri = lax.broadcasted_iota(jnp.int32, (c, c), 0)
    ci = lax.broadcasted_iota(jnp.int32, (c, c), 1)
    tri = ci <= ri
    strict = ci < ri
    ltri = jnp.where(tri, 1.0, 0.0)
    probs = [(cc, h) for cc in range(chunks) for h in range(HEADS)]
    rows = lambda cc: slice(cc * c, (cc + 1) * c)
    hsl = lambda h: slice(h * GDN_DK, (h + 1) * GDN_DK)
    gbs = [gb_ref[rows(cc), :] for cc in range(chunks)]
    gcs = [_mm(ltri, gb[:, 0:HEADS], HI) for gb in gbs]
    gcts = [g.T for g in gcs]
    q = [q_ref[rows(cc), hsl(h)] for cc, h in probs]
    k = [k_ref[rows(cc), hsl(h)] for cc, h in probs]
    v = [v_ref[rows(cc), hsl(h)] for cc, h in probs]
    gcol = [gcs[cc][:, h:h + 1] for cc, h in probs]
    grow = [gcts[cc][h:h + 1, :] for cc, h in probs]
    glast = [gcs[cc][c - 1:c, h:h + 1] for cc, h in probs]
    bcol = [gbs[cc][:, HEADS + h:HEADS + h + 1] for cc, h in probs]
    n = len(probs)
    egc = [jnp.exp(g) for g in gcol]
    decay = [jnp.where(tri, jnp.exp(jnp.where(tri, gcol[i] - grow[i], 0.0)), 0.0) for i in range(n)]
    kb = [k[i] * bcol[i] for i in range(n)]
    k16 = [a.astype(BF16) for a in k]
    kb16 = [a.astype(BF16) for a in kb]
    a_neg = [jnp.where(strict, -(_nt(kb16[i], k16[i]) * decay[i]), 0.0) for i in range(n)]
    x = [jnp.concatenate([v[i] * bcol[i], kb[i] * egc[i]], axis=1) for i in range(n)]
    def mm_split_rhs(a16_, rhs):
        hi = rhs.astype(BF16)
        lo = (rhs - hi.astype(F32)).astype(BF16)
        return _mm(a16_, hi) + _mm(a16_, lo)

    for lvl in range(6):
        a16 = [a.astype(BF16) for a in a_neg]
        if lvl < 5:
            r = [mm_split_rhs(a16[i], jnp.concatenate([a_neg[i], x[i]], axis=1)) for i in range(n)]
            a_neg = [ri_[:, 0:c] for ri_ in r]
            x = [x[i] + r[i][:, c:] for i in range(n)]
        else:
            x = [x[i] + mm_split_rhs(a16[i], x[i]) for i in range(n)]
    u = [xi[:, 0:GDN_DV] for xi in x]
    w16 = [xi[:, GDN_DV:].astype(BF16) for xi in x]
    attn16 = [(_nt(q[i].astype(BF16), k16[i]) * decay[i]).astype(BF16) for i in range(n)]
    qe16 = [(q[i] * egc[i]).astype(BF16) for i in range(n)]
    kd16 = [(k[i] * jnp.exp(glast[i] - gcol[i])).astype(BF16) for i in range(n)]
    eglast = [jnp.exp(g) for g in glast]
    state = [s_scr[h] for h in range(HEADS)]
    for cc in range(chunks):
        ids = [cc * HEADS + h for h in range(HEADS)]
        s16 = [s.astype(BF16) for s in state]
        v_new = [u[i] - _mm(w16[i], s16[h]) for h, i in enumerate(ids)]
        vn16 = [a.astype(BF16) for a in v_new]
        o = [_mm(qe16[i], s16[h]) + _mm(attn16[i], vn16[h]) for h, i in enumerate(ids)]
        state = [state[h] * eglast[i] + _tn(kd16[i], vn16[h]) for h, i in enumerate(ids)]
        o = [oh * lax.rsqrt(jnp.mean(oh * oh, axis=-1, keepdims=True) + NORM_EPS) * og_ref[...] for oh in o]
        o_ref[rows(cc), :] = jnp.concatenate(o, axis=1) * _silu(z_ref[rows(cc), :])
    for h in range(HEADS):
        s_scr[h] = state[h]


def _gdn_scan(qn, kn, vv, gb, proj, lw, bsz, seq, chunks=4):
    t = qn.shape[0]
    ct = chunks * GDN_CHUNK
    ns = seq // ct
    tok = lambda w: pl.BlockSpec((ct, w), lambda b, i: (b * ns + i, 0))
    return pl.pallas_call(
        functools.partial(_gdn_scan_kernel, chunks=chunks),
        grid=(bsz, ns),
        in_specs=[tok(512), tok(512), tok(512), tok(128),
                  pl.BlockSpec((ct, 512), lambda b, i: (b * ns + i, COL_GQ // 512 + 3)),
                  pl.BlockSpec((1, GDN_DV), lambda b, i: (0, 0))],
        out_specs=tok(512),
        out_shape=jax.ShapeDtypeStruct((t, 512), F32),
        scratch_shapes=[pltpu.VMEM((HEADS, GDN_DK, GDN_DV), F32)],
        compiler_params=_cp(("arbitrary", "arbitrary")),
        name="gdn_scan",
    )(qn, kn, vv, gb, proj, lw["gdn_o_gain"])


def _merge_kernel(oa_ref, ob_ref, oc_ref, g0_ref, g1_ref, g2_ref, x_ref, ga_ref, wb_ref, wo_ref, o_ref):
    y = None
    for o_n, g_n, n in ((oa_ref, g0_ref, 0), (ob_ref, g1_ref, 1), (oc_ref, g2_ref, 2)):
        term = jax.nn.sigmoid(g_n[...]) * _mm(o_n[...].astype(BF16), wb_ref[n])
        y = term if y is None else y + term
    o_ref[...] = x_ref[...] + ga_ref[0] * _mm(y.astype(BF16), wo_ref[...])


def _merge(o_mla, o_gdn, o_moba, proj, x, g_a, lw, seq, tm=512):
    t, d = x.shape
    bsz = g_a.shape[0]
    tok = lambda w: pl.BlockSpec((tm, w), lambda i: (i, 0))
    gcol = COL_GATE // d
    return pl.pallas_call(
        _merge_kernel,
        grid=(t // tm,),
        in_specs=[tok(BRANCH_W), tok(BRANCH_W), tok(BRANCH_W),
                  pl.BlockSpec((tm, d), lambda i: (i, gcol)),
                  pl.BlockSpec((tm, d), lambda i: (i, gcol + 1)),
                  pl.BlockSpec((tm, d), lambda i: (i, gcol + 2)),
                  tok(d),
                  pl.BlockSpec((1, 1, d), lambda i: ((i * tm) // seq, 0, 0)),
                  pl.BlockSpec((N_BRANCH, BRANCH_W, d), lambda i: (0, 0, 0)),
                  pl.BlockSpec((d, d), lambda i: (0, 0))],
        out_specs=tok(d),
        out_shape=jax.ShapeDtypeStruct((t, d), F32),
        compiler_params=_cp(("arbitrary",)),
        name="branch_merge",
    )(o_mla, o_gdn, o_moba, proj, proj, proj, x, g_a.reshape(bsz, 1, d), lw["w_branch"], lw["w_out"])


def _topk_rows(s, k):
    n = s.shape[0]
    ri = lax.broadcasted_iota(jnp.int32, s.shape, 0).astype(F32)
    vals, idxs = [], []
    for _ in range(k):
        mx = jnp.max(s, axis=0, keepdims=True)
        idx = jnp.min(jnp.where(s == mx, ri, float(n)), axis=0, keepdims=True)
        vals.append(mx)
        idxs.append(idx)
        s = jnp.where(ri == idx, -jnp.inf, s)
    return jnp.concatenate(vals, axis=0), jnp.concatenate(idxs, axis=0).astype(jnp.int32)


def _peer_select_kernel(qry_ref, keys_ref, ids_out, gates_out):
    kk = PEER_TOPK
    ids_rows, gate_rows = [], []
    for h in range(PEER_HEADS):
        halves = []
        for p in range(2):
            g = h * 2 + p
            s_t = _nt(keys_ref[g], qry_ref[:, g * 128:(g + 1) * 128], HI)
            halves.append(_topk_rows(s_t, kk))
        (v1, i1), (v2, i2) = halves
        cand, a_of, b_of = [], [], []
        for a in range(kk):
            nb = kk // (a + 1)
            cand.append(v1[a:a + 1, :] + v2[0:nb, :])
            a_of += [a] * nb
            b_of += list(range(nb))
        cand = jnp.concatenate(cand, axis=0)
        top_s, top_r = _topk_rows(cand, kk)
        a_sel = jnp.zeros_like(top_r)
        b_sel = jnp.zeros_like(top_r)
        for r, (a, b) in enumerate(zip(a_of, b_of)):
            hit = top_r == r
            a_sel = jnp.where(hit, a, a_sel)
            b_sel = jnp.where(hit, b, b_sel)
        e1 = jnp.zeros_like(top_r)
        e2 = jnp.zeros_like(top_r)
        for a in range(kk):
            e1 = jnp.where(a_sel == a, i1[a:a + 1, :], e1)
            e2 = jnp.where(b_sel == a, i2[a:a + 1, :], e2)
        ids_rows.append(e1 * PEER_NKEYS + e2)
        ex = jnp.exp(top_s - top_s[0:1, :])
        gate_rows.append(ex / jnp.sum(ex, axis=0, keepdims=True))
    ids_out[...] = jnp.concatenate(ids_rows, axis=0).T
    gates_out[...] = jnp.concatenate(gate_rows, axis=0).T


def _peer_select(qry, sub_keys, tm=256):
    t = qry.shape[0]
    return pl.pallas_call(
        _peer_select_kernel,
        grid=(t // tm,),
        in_specs=[pl.BlockSpec((tm, 2 * PEER_HEADS * 128), lambda i: (i, 0)),
                  pl.BlockSpec((2 * PEER_HEADS, PEER_NKEYS, 128), lambda i: (0, 0, 0))],
        out_specs=[pl.BlockSpec((tm, PEER_SEL), lambda i: (i, 0)),
                   pl.BlockSpec((tm, PEER_SEL), lambda i: (i, 0))],
        out_shape=[jax.ShapeDtypeStruct((t, PEER_SEL), jnp.int32), jax.ShapeDtypeStruct((t, PEER_SEL), F32)],
        compiler_params=_cp(("arbitrary",)),
        name="peer_select",
    )(qry, sub_keys)


def _peer_apply_kernel(ids_ref, gates_ref, h_ref, x_ref, gf_ref, uv_hbm, o_ref, buf0, buf1, sems, *, tt):
    s = pl.program_id(0)
    ns = pl.num_programs(0) - 1
    slot = s % 2
    buf = (buf0, buf1)

    def start_rows(dst_slot):
        for t in range(tt):
            for e in range(PEER_SEL):
                pltpu.make_async_copy(uv_hbm.at[ids_ref[t, e]], buf[dst_slot].at[t, e],
                                      sems.at[dst_slot]).start(priority=e % 2)

    def wait_rows(src_slot):
        pltpu.make_async_copy(buf[1 - src_slot], buf[src_slot], sems.at[src_slot]).wait()

    def reduce_tile(src_slot):
        src = buf[src_slot]
        nsub = EXPERT_ROWS // 2
        prod = src[:, :, 0:nsub, :].astype(F32) * h_ref[...][:, None, :, :]
        act = jnp.sum(jnp.sum(prod, axis=2), axis=-1)
        gel = 0.5 * act * (1.0 + lax.erf(act * (2.0 ** -0.5)))
        wgt = (gates_ref[...] * gel)[:, :, None, None]
        out = jnp.sum(wgt * src[:, :, nsub:, :].astype(F32), axis=1)
        o_ref[...] = x_ref[...] + gf_ref[...] * out

    def steady(parity):
        wait_rows(1 - parity)
        start_rows(parity)
        reduce_tile(1 - parity)

    def last(parity):
        wait_rows(1 - parity)
        reduce_tile(1 - parity)

    for parity in range(2):
        pl.when((s >= 1) & (s < ns) & (slot == parity))(functools.partial(steady, parity))
    pl.when(s == 0)(functools.partial(start_rows, 0))
    for parity in range(2):
        pl.when((s == ns) & (slot == parity))(functools.partial(last, parity))


def _peer_apply(ids, gates, h2, x, g_f, uv, seq, tt=16):
    t, d = x.shape
    bsz = g_f.shape[0]
    ns = t // tt
    sub = d // 128
    cur = lambda i: jnp.maximum(i - 1, 0)
    tile3 = pl.BlockSpec((tt, sub, 128), lambda i: (cur(i), 0, 0))
    out = pl.pallas_call(
        functools.partial(_peer_apply_kernel, tt=tt),
        grid=(ns + 1,),
        in_specs=[pl.BlockSpec((tt, PEER_SEL), lambda i: (jnp.minimum(i, ns - 1), 0), memory_space=pltpu.SMEM),
                  pl.BlockSpec((tt, PEER_SEL), lambda i: (cur(i), 0)), tile3, tile3,
                  pl.BlockSpec((1, sub, 128), lambda i: ((cur(i) * tt) // seq, 0, 0)),
                  pl.BlockSpec(memory_space=pl.ANY)],
        out_specs=tile3,
        out_shape=jax.ShapeDtypeStruct((t, sub, 128), F32),
        scratch_shapes=[pltpu.VMEM((tt, PEER_SEL, EXPERT_ROWS, 128), uv.dtype),
                        pltpu.VMEM((tt, PEER_SEL, EXPERT_ROWS, 128), uv.dtype),
                        pltpu.SemaphoreType.DMA((2,))],
        compiler_params=_cp(("arbitrary",)),
        name="peer_apply",
    )(ids, gates, h2.reshape(t, sub, 128), x.reshape(t, sub, 128), g_f.reshape(bsz, sub, 128), uv)
    return out.reshape(t, d)


def _prep_layer_weights(l, w_in, mla_q_gain, mla_w_uq, mla_kv_gain, mla_w_ukv, mla_qn_gain, mla_kn_gain,
                        gdn_conv_w, gdn_a_log, gdn_dt_bias, gdn_o_gain, moba_qn_gain, moba_kn_gain,
                        w_branch, w_out, peer_w_query, peer_sub_keys):
    d = D_MODEL
    w = w_in[l]
    o = 0
    parts = {}
    for name, width in (("cq", 256), ("ckv", 128), ("kr", 32), ("gq", 512), ("gk", 512), ("gv", 512), ("gz", 512),
                        ("ga", 8), ("gb", 8), ("mq", 512), ("mk", 512), ("mv", 512), ("gate", 3072)):
        parts[name] = w[:, o:o + width]
        o += width
    w_in_p = jnp.concatenate(
        [parts["cq"], parts["ckv"], parts["kr"], parts["ga"], parts["gb"], jnp.zeros((d, 80), F32),
         parts["gq"], parts["gk"], parts["gv"], parts["gz"], parts["mv"], parts["mq"], parts["mk"], parts["gate"]],
        axis=1)
    w_in_hi = w_in_p.astype(BF16)
    sel = slice(COL_MQ, COL_MQ + 1024)
    w_in_lo = (w_in_p[:, sel] - w_in_hi[:, sel].astype(F32)).astype(BF16)
    wq_hi = peer_w_query[l].astype(BF16)
    wq_lo = (peer_w_query[l] - wq_hi.astype(F32)).astype(BF16)

    def pad_heads(m, width):
        r = m.shape[0]
        return jnp.pad(m.reshape(r, HEADS, width), ((0, 0), (0, 0), (0, HEAD_PAD - width))).reshape(r, HEADS * HEAD_PAD)

    ukv = mla_w_ukv[l].reshape(MLA_KV_RANK, HEADS, MLA_NOPE + MLA_V)
    pad1 = lambda g, n: jnp.pad(g, (0, n - g.shape[0])).reshape(1, n)
    lane = jnp.arange(HEAD_PAD)
    half = MLA_ROPE // 2
    inv_freq = jnp.where((lane >= MLA_NOPE) & (lane < MLA_QK),
                         ROPE_THETA ** (-((lane - MLA_NOPE) % half).astype(F32) / half), 0.0).reshape(1, HEAD_PAD)
    head_sum = jnp.repeat(jnp.eye(HEADS, dtype=F32), 64, axis=0)
    return dict(
        w_in=w_in_hi, w_in_lo=w_in_lo,
        mla_q_gain=mla_q_gain[l].reshape(1, -1),
        mla_w_uq=pad_heads(mla_w_uq[l], MLA_QK).astype(BF16),
        mla_kv_gain=mla_kv_gain[l].reshape(1, -1),
        mla_w_uk=pad_heads(ukv[:, :, :MLA_NOPE].reshape(MLA_KV_RANK, -1), MLA_NOPE).astype(BF16),
        mla_w_uv=ukv[:, :, MLA_NOPE:].reshape(MLA_KV_RANK, -1).astype(BF16),
        mla_qn_gain=pad1(mla_qn_gain[l], HEAD_PAD),
        mla_kn_gain=pad1(mla_kn_gain[l], HEAD_PAD),
        rope_inv_freq=inv_freq,
        conv_wq=gdn_conv_w[l][:, 0:512], conv_wk=gdn_conv_w[l][:, 512:1024], conv_wv=gdn_conv_w[l][:, 1024:1536],
        gdn_a_log=gdn_a_log[l].reshape(1, -1), gdn_dt_bias=gdn_dt_bias[l].reshape(1, -1),
        gdn_o_gain=gdn_o_gain[l].reshape(1, -1),
        moba_qn_gain=jnp.tile(moba_qn_gain[l], HEADS).reshape(1, -1),
        moba_kn_gain=jnp.tile(moba_kn_gain[l], HEADS).reshape(1, -1),
        head_sum=head_sum, head_expand=head_sum.T,
        w_branch=w_branch[l].astype(BF16), w_out=w_out[l].astype(BF16),
        peer_w_query=wq_hi, peer_w_query_lo=wq_lo,
        peer_sub_keys=peer_sub_keys[l].reshape(2 * PEER_HEADS, PEER_NKEYS, -1),
    )


def kernel(x, c, positions, w_mod, b_mod, w_in, mla_q_gain, mla_w_uq, mla_kv_gain, mla_w_ukv, mla_qn_gain, mla_kn_gain, gdn_conv_w, gdn_a_log, gdn_dt_bias, gdn_o_gain, moba_qn_gain, moba_kn_gain, w_branch, w_out, peer_w_query, peer_sub_keys, peer_u, peer_v):
    bsz, seq, d = x.shape
    t = bsz * seq
    depth = w_mod.shape[0]
    xt = x.reshape(t, d)
    pos = positions.reshape(t, 1)
    mod = _mod_all(c, w_mod, b_mod)
    for l in range(depth):
        lw = _prep_layer_weights(l, w_in, mla_q_gain, mla_w_uq, mla_kv_gain, mla_w_ukv, mla_qn_gain, mla_kn_gain,
                                 gdn_conv_w, gdn_a_log, gdn_dt_bias, gdn_o_gain, moba_qn_gain, moba_kn_gain,
                                 w_branch, w_out, peer_w_query, peer_sub_keys)
        sh_a, sc_a, g_a, sh_f, sc_f, g_f = [mod[l, :, i * d:(i + 1) * d] for i in range(6)]
        proj, _ = _modulate_matmul(xt, sh_a, sc_a, lw["w_in"], lw["w_in_lo"], COL_MQ // 1024, seq)
        q, k, v = _mla_prep(proj, pos, lw, seq)
        o_mla = _causal_attention(q, k, v, bsz, seq)
        q, k, v = _moba_prep(proj, lw, seq)
        o_moba = _causal_attention(q, k, v, bsz, seq)
        qn, kn, vv, gb = _gdn_prep(proj, lw, seq)
        o_gdn = _gdn_scan(qn, kn, vv, gb, proj, lw, bsz, seq)
        xt = _merge(o_mla, o_gdn, o_moba, proj, xt, g_a, lw, seq)
        qry, h2 = _modulate_matmul(xt, sh_f, sc_f, lw["peer_w_query"], lw["peer_w_query_lo"], 0, seq)
        ids, gates = _peer_select(qry, lw["peer_sub_keys"])
        uv = jnp.concatenate([peer_u[l], peer_v[l]], axis=1).reshape(-1, EXPERT_ROWS, 128).astype(BF16)
        xt = _peer_apply(ids, gates, h2, xt, g_f, uv, seq)
    return xt.reshape(bsz, seq, d)
```
